```python
import jax, jax.numpy as jnp
from jax import lax
import numpy as np

D_MODEL = 1024
BATCH = 2
SEQ = 8192
DEPTH = 4
DEC_BATCH = 128
DEC_SEQ = 8
PAST_LEN = 2048
PAGE_SIZE = 128

WA = D_MODEL // 2
A_GROUPS = 8
CHUNK = 128
MOBA_HEADS = 8
HEAD_DIM = 64
WB = MOBA_HEADS * HEAD_DIM
MOBA_BLOCK = 256
MOBA_TOPK = 3
QBLOCK = 64
ROPE_THETA = 10000.0
EPS = 1e-6
SPLIT_SIZES = (WA, WA, WA, WB, WB, WB, WB, D_MODEL, D_MODEL)
P_WIDTH = sum(SPLIT_SIZES)

kernel_name = 'hybrid_chunkmlp_moba_decoder_step'


def rmsnorm(x, g):
    xf = x.astype(jnp.float32)
    xf = xf * lax.rsqrt(jnp.mean(xf * xf, axis=-1, keepdims=True) + EPS)
    return (xf * g.astype(jnp.float32)).astype(x.dtype)


def rope(x, pos):
    half = HEAD_DIM // 2
    inv = jnp.power(jnp.float32(ROPE_THETA), -jnp.arange(half, dtype=jnp.float32) * (2.0 / HEAD_DIM))
    ang = pos.astype(jnp.float32)[:, None] * inv[None, :]
    cos = jnp.cos(ang)[None, :, None, :]
    sin = jnp.sin(ang)[None, :, None, :]
    xf = x.astype(jnp.float32)
    x1, x2 = xf[..., :half], xf[..., half:]
    return jnp.concatenate([x1 * cos - x2 * sin, x1 * sin + x2 * cos], axis=-1).astype(x.dtype)


def project(x, pos, norm_g, w_in, v_norm_g, q_norm_g, k_norm_g):
    B, S, _ = x.shape
    p = rmsnorm(x, norm_g) @ w_in
    offs = np.cumsum(SPLIT_SIZES)[:-1].tolist()
    uA, vA, zA, q, k, vB, zB, gA, gB = jnp.split(p, offs, axis=-1)
    vA = rmsnorm(vA, v_norm_g)
    q = rope(rmsnorm(q.reshape(B, S, MOBA_HEADS, HEAD_DIM), q_norm_g), pos)
    k = rope(rmsnorm(k.reshape(B, S, MOBA_HEADS, HEAD_DIM), k_norm_g), pos)
    vB = vB.reshape(B, S, MOBA_HEADS, HEAD_DIM)
    return uA, vA, zA, q, k, vB, zB, gA, gB


def chunk_mix(u, v, w_s, b_s):
    B, S, W = v.shape
    L = min(S, CHUNK)
    n = S // L
    ws = jnp.tril(w_s[:, :L, :L])
    vg = v.reshape(B, n, L, A_GROUPS, W // A_GROUPS)
    mixed = jnp.einsum('gts,bnsgc->bntgc', ws, vg) + b_s[:, :L].T[None, None, :, :, None]
    return u * mixed.reshape(B, S, W)


def merge(x, yA, yB, zA, zB, gA, gB, w_pa, w_pb, w_o):
    yA = yA * jax.nn.silu(zA)
    yB = yB * jax.nn.silu(zB)
    m = jax.nn.sigmoid(gA) * (yA @ w_pa) + jax.nn.sigmoid(gB) * (yB @ w_pb)
    return x + m @ w_o


def moba_attend(q, q_pos, n_past, k_own, v_own, own_pos, k_blk, v_blk, k_mean):
    B, H, Q, _ = q.shape
    scale = HEAD_DIM ** -0.5
    s_own = jnp.einsum('bhqd,bhkd->bhqk', q, k_own).astype(jnp.float32) * scale
    s_own = jnp.where(own_pos[None, :] <= q_pos[:, None], s_own, -jnp.inf)
    if k_blk is None:
        p = jax.nn.softmax(s_own, axis=-1).astype(v_own.dtype)
        return jnp.einsum('bhqk,bhkd->bhqd', p, v_own)
    l_own = k_own.shape[2]
    nf = k_blk.shape[2]
    topk = min(MOBA_TOPK, nf)
    gate = jnp.einsum('bhqd,bhnd->bhqn', q, k_mean).astype(jnp.float32)
    gate = jnp.where(jnp.arange(nf)[None, :] < n_past[:, None], gate, -jnp.inf)
    _, idx = lax.top_k(gate, topk)
    bi = jnp.arange(B)[:, None, None, None]
    hi = jnp.arange(H)[None, :, None, None]
    k_sel = k_blk[bi, hi, idx]
    v_sel = v_blk[bi, hi, idx]
    s_sel = jnp.einsum('bhqd,bhqnkd->bhqnk', q, k_sel).astype(jnp.float32) * scale
    valid = jnp.arange(topk)[None, :] < n_past[:, None]
    s_sel = jnp.where(valid[None, None, :, :, None], s_sel, -jnp.inf)
    s = jnp.concatenate([s_own, s_sel.reshape(B, H, Q, topk * MOBA_BLOCK)], axis=-1)
    p = jax.nn.softmax(s, axis=-1).astype(v_own.dtype)
    p_sel = p[..., l_own:].reshape(B, H, Q, topk, MOBA_BLOCK)
    return (jnp.einsum('bhqk,bhkd->bhqd', p[..., :l_own], v_own)
            + jnp.einsum('bhqnk,bhqnkd->bhqd', p_sel, v_sel))


def moba_prompt(q, k, v):
    B, S, H, hd = q.shape
    qh, kh, vh = (t.transpose(0, 2, 1, 3) for t in (q, k, v))
    nb = -(-S // MOBA_BLOCK)
    pad = nb * MOBA_BLOCK - S
    kp = jnp.pad(kh, ((0, 0), (0, 0), (0, pad), (0, 0)))
    vp = jnp.pad(vh, ((0, 0), (0, 0), (0, pad), (0, 0)))
    k_blk = kp.reshape(B, H, nb, MOBA_BLOCK, hd)
    v_blk = vp.reshape(B, H, nb, MOBA_BLOCK, hd)
    k_mean = jnp.mean(k_blk.astype(jnp.float32), axis=3).astype(k.dtype)

    def step(c):
        start = c * QBLOCK
        qc = lax.dynamic_slice_in_dim(qh, start, QBLOCK, axis=2)
        q_pos = start + jnp.arange(QBLOCK)
        own_start = (start // MOBA_BLOCK) * MOBA_BLOCK
        k_own = lax.dynamic_slice_in_dim(kp, own_start, MOBA_BLOCK, axis=2)
        v_own = lax.dynamic_slice_in_dim(vp, own_start, MOBA_BLOCK, axis=2)
        own_pos = own_start + jnp.arange(MOBA_BLOCK)
        return moba_attend(qc, q_pos, q_pos // MOBA_BLOCK, k_own, v_own, own_pos, k_blk, v_blk, k_mean)

    out = lax.map(step, jnp.arange(S // QBLOCK))
    return out.transpose(1, 0, 3, 2, 4).reshape(B, S, H * hd)


def moba_sample(q, k, v, past_k, past_v, past_len):
    DB, T, H, hd = q.shape
    qh, kh, vh = (t.transpose(0, 2, 1, 3) for t in (q, k, v))
    pk = past_k.transpose(0, 2, 1, 3)
    pv = past_v.transpose(0, 2, 1, 3)
    nf = past_len // MOBA_BLOCK
    own_start = nf * MOBA_BLOCK
    if nf > 0:
        k_blk = pk[:, :, :own_start].reshape(DB, H, nf, MOBA_BLOCK, hd)
        v_blk = pv[:, :, :own_start].reshape(DB, H, nf, MOBA_BLOCK, hd)
        k_mean = jnp.mean(k_blk.astype(jnp.float32), axis=3).astype(k.dtype)
    else:
        k_blk = v_blk = k_mean = None
    k_own = jnp.concatenate([pk[:, :, own_start:], kh], axis=2)
    v_own = jnp.concatenate([pv[:, :, own_start:], vh], axis=2)
    own_pos = jnp.arange(own_start, past_len + T)

    def step(t):
        qt = lax.dynamic_slice_in_dim(qh, t, 1, axis=2)
        q_pos = past_len + t + jnp.arange(1)
        return moba_attend(qt, q_pos, q_pos // MOBA_BLOCK, k_own, v_own, own_pos, k_blk, v_blk, k_mean)[:, :, 0]

    out = lax.map(step, jnp.arange(T))
    return out.transpose(1, 0, 2, 3).reshape(DB, T, H * hd)


def setup_inputs(seed: int = 0) -> dict:
    key = jax.random.key(seed)
    ks = jax.random.split(key, 16)
    n_pages = PAST_LEN // PAGE_SIZE
    n_used = DEC_BATCH * n_pages
    n_pool = (n_used * 5) // 4
    f32 = jnp.float32
    page_table = jax.random.permutation(ks[4], n_pool)[:n_used].reshape(DEC_BATCH, n_pages).astype(jnp.int32)
    return {
        'x_prompt': jax.random.normal(ks[0], (BATCH, SEQ, D_MODEL), f32),
        'x_sample': jax.random.normal(ks[1], (DEC_BATCH, DEC_SEQ, D_MODEL), f32),
        'cache_k': jax.random.normal(ks[2], (DEPTH, n_pool, PAGE_SIZE, MOBA_HEADS, HEAD_DIM), f32),
        'cache_v': jax.random.normal(ks[3], (DEPTH, n_pool, PAGE_SIZE, MOBA_HEADS, HEAD_DIM), f32),
        'page_table': page_table,
        'norm_g': 1.0 + 0.02 * jax.random.normal(ks[5], (DEPTH, D_MODEL), f32),
        'w_in': jax.random.normal(ks[6], (DEPTH, D_MODEL, P_WIDTH), f32) * D_MODEL ** -0.5,
        'v_norm_g': 1.0 + 0.02 * jax.random.normal(ks[7], (DEPTH, WA), f32),
        'w_s': jax.random.normal(ks[8], (DEPTH, A_GROUPS, CHUNK, CHUNK), f32) * CHUNK ** -0.5,
        'b_s': 1.0 + 0.1 * jax.random.normal(ks[9], (DEPTH, A_GROUPS, CHUNK), f32),
        'q_norm_g': 1.0 + 0.02 * jax.random.normal(ks[10], (DEPTH, HEAD_DIM), f32),
        'k_norm_g': 1.0 + 0.02 * jax.random.normal(ks[11], (DEPTH, HEAD_DIM), f32),
        'w_pa': jax.random.normal(ks[12], (DEPTH, WA, D_MODEL), f32) * WA ** -0.5,
        'w_pb': jax.random.normal(ks[13], (DEPTH, WB, D_MODEL), f32) * WB ** -0.5,
        'w_o': jax.random.normal(ks[14], (DEPTH, D_MODEL, D_MODEL), f32) * D_MODEL ** -0.5,
    }


def reference(x_prompt, x_sample, cache_k, cache_v, page_table, norm_g, w_in, v_norm_g, w_s, b_s,
              q_norm_g, k_norm_g, w_pa, w_pb, w_o):
    xp, xs = x_prompt, x_sample
    S = xp.shape[1]
    DB, T, _ = xs.shape
    past_len = page_table.shape[1] * cache_k.shape[2]
    pos_p = jnp.arange(S)
    pos_s = past_len + jnp.arange(T)
    kp_l, vp_l, cp_l, ks_l, vs_l, cs_l = [], [], [], [], [], []
    for l in range(DEPTH):
        uA, vA, zA, q, k, vB, zB, gA, gB = project(xp, pos_p, norm_g[l], w_in[l], v_norm_g[l], q_norm_g[l], k_norm_g[l])
        yA = chunk_mix(uA, vA, w_s[l], b_s[l])
        yB = moba_prompt(q, k, vB)
        xp = merge(xp, yA, yB, zA, zB, gA, gB, w_pa[l], w_pb[l], w_o[l])
        kp_l.append(k)
        vp_l.append(vB)
        cp_l.append(vA[:, S - CHUNK:])
        uA, vA, zA, q, k, vB, zB, gA, gB = project(xs, pos_s, norm_g[l], w_in[l], v_norm_g[l], q_norm_g[l], k_norm_g[l])
        yA = chunk_mix(uA, vA, w_s[l], b_s[l])
        past_k = cache_k[l][page_table].reshape(DB, past_len, MOBA_HEADS, HEAD_DIM)
        past_v = cache_v[l][page_table].reshape(DB, past_len, MOBA_HEADS, HEAD_DIM)
        yB = moba_sample(q, k, vB, past_k, past_v, past_len)
        xs = merge(xs, yA, yB, zA, zB, gA, gB, w_pa[l], w_pb[l], w_o[l])
        ks_l.append(k)
        vs_l.append(vB)
        cs_l.append(vA)
    return (xp, xs, jnp.stack(kp_l), jnp.stack(vp_l), jnp.stack(cp_l),
            jnp.stack(ks_l), jnp.stack(vs_l), jnp.stack(cs_l))
```

```python
import functools

import jax
import jax.numpy as jnp
import numpy as np
from jax import lax
from jax.experimental import pallas as pl
from jax.experimental.pallas import tpu as pltpu

D_MODEL = 1024
WA = 512
A_GROUPS = 8
CHUNK = 128
HEADS = 8
HEAD_DIM = 64
WB = HEADS * HEAD_DIM
BLOCK = 256
TOPK = 3
ROPE_THETA = 10000.0
EPS = 1e-6
P_WIDTH = 3 * WA + 4 * WB + 2 * D_MODEL
OFF = np.cumsum((0, WA, WA, WA, WB, WB, WB, WB, D_MODEL)).tolist()

LANES = 128
VMEM_LIMIT = 56 * 1024 * 1024

F32 = jnp.float32
BF16 = jnp.bfloat16
NEG_INF = float("-inf")


def _dot(a, b):
    return jnp.dot(a, b, preferred_element_type=F32)


def _dot_nt(a, b):
    return lax.dot_general(a, b, (((1,), (1,)), ((), ())), preferred_element_type=F32)


def _split_bf16(x):
    hi = x.astype(BF16)
    lo = (x - hi.astype(F32)).astype(BF16)
    return hi, lo


def _params(n_axes):
    return pltpu.CompilerParams(dimension_semantics=("arbitrary",) * n_axes, vmem_limit_bytes=VMEM_LIMIT)


def _head_norm_rope(p, gain, hsum, cos, sin, first_half):
    ms = _dot((p * p).astype(BF16), hsum) * (1.0 / HEAD_DIM)
    xn = p * lax.rsqrt(ms + EPS) * gain
    partner = jnp.where(first_half, pltpu.roll(xn, WB - HEAD_DIM // 2, 1), pltpu.roll(xn, HEAD_DIM // 2, 1))
    return xn * cos + partner * sin


def _proj_kernel(prompt, x_ref, ng_ref, w_ref, vg_ref, qg_ref, kg_ref, cos_ref, sin_ref, hsum_ref, *outs):
    if prompt:
        u_ref, va_ref, za_ref, q_ref, k_ref, vb_ref, zb_ref, ga_ref, gb_ref, kbf_ref, vt_ref, km_ref = outs
    else:
        u_ref, va_ref, za_ref, q_ref, k_ref, vb_ref, zb_ref, ga_ref, gb_ref = outs
    x = x_ref[...]
    ms = jnp.mean(x * x, axis=-1, keepdims=True)
    h = (x * lax.rsqrt(ms + EPS) * ng_ref[...]).astype(BF16)

    def col(i, width):
        return _dot(h, w_ref[:, OFF[i]:OFF[i] + width])

    u_ref[...] = col(0, WA)
    va = col(1, WA)
    va_ref[...] = va * lax.rsqrt(jnp.mean(va * va, axis=-1, keepdims=True) + EPS) * vg_ref[...]
    za_ref[...] = col(2, WA)

    reps = WB // LANES
    cos = jnp.concatenate([cos_ref[...]] * reps, axis=1)
    sin = jnp.concatenate([sin_ref[...]] * reps, axis=1)
    lane = lax.broadcasted_iota(jnp.int32, (1, WB), 1)
    first_half = (lane % HEAD_DIM) < (HEAD_DIM // 2)
    hsum = hsum_ref[...]
    q_ref[...] = _head_norm_rope(col(3, WB), qg_ref[...], hsum, cos, sin, first_half)
    k = _head_norm_rope(col(4, WB), kg_ref[...], hsum, cos, sin, first_half)
    k_ref[...] = k
    vb = col(5, WB)
    vb_ref[...] = vb
    zb_ref[...] = col(6, WB)
    ga_ref[...] = col(7, D_MODEL)
    gb_ref[...] = col(8, D_MODEL)
    if prompt:
        kbf_ref[...] = k.astype(BF16)
        vt_ref[0, 0] = vb.T.astype(BF16)
        km_ref[0] = jnp.mean(k, axis=0, keepdims=True)


def _proj(prompt, x2d, seq, ng, w, vg, qg, kg, cos, sin, hsum):
    rows = x2d.shape[0]
    tm = BLOCK
    nt = rows // tm
    tiles_per_seq = seq // tm if prompt else nt
    row = lambda w_: pl.BlockSpec((tm, w_), lambda i: (i, 0))
    full = lambda a: pl.BlockSpec(a.shape, lambda i: (0,) * a.ndim)
    tab = pl.BlockSpec((tm, LANES), lambda i: (i % tiles_per_seq, 0))
    out_shape = [jax.ShapeDtypeStruct((rows, w_), F32) for w_ in (WA, WA, WA, WB, WB, WB, WB, D_MODEL, D_MODEL)]
    out_specs = [row(w_) for w_ in (WA, WA, WA, WB, WB, WB, WB, D_MODEL, D_MODEL)]
    if prompt:
        out_shape += [jax.ShapeDtypeStruct((rows, WB), BF16),
                      jax.ShapeDtypeStruct((nt, 1, WB, tm), BF16),
                      jax.ShapeDtypeStruct((nt, 1, WB), F32)]
        out_specs += [row(WB),
                      pl.BlockSpec((1, 1, WB, tm), lambda i: (i, 0, 0, 0)),
                      pl.BlockSpec((1, 1, WB), lambda i: (i, 0, 0))]
    return pl.pallas_call(
        functools.partial(_proj_kernel, prompt),
        out_shape=out_shape,
        grid=(nt,),
        in_specs=[row(D_MODEL), full(ng), full(w), full(vg), full(qg), full(kg), tab, tab, full(hsum)],
        out_specs=out_specs,
        compiler_params=_params(1),
        name="proj_prompt" if prompt else "proj_sample",
    )(x2d, ng, w, vg, qg, kg, cos, sin, hsum)


def _top_blocks_bias(g, valid, axis):
    idx = lax.broadcasted_iota(jnp.int32, g.shape, axis).astype(F32)
    n = float(g.shape[axis])
    g = jnp.where(valid, g, NEG_INF)
    sel = jnp.zeros(g.shape, jnp.bool_)
    for _ in range(min(TOPK, n)):
        m = jnp.max(g, axis=axis, keepdims=True)
        first = jnp.min(jnp.where(g == m, idx, n), axis=axis, keepdims=True)
        pick = (idx == first) & (m > NEG_INF)
        sel = sel | pick
        g = jnp.where(pick, NEG_INF, g)
    return jnp.where(sel, 0.0, NEG_INF).astype(F32)


def _block_diag_rows(km, heads):
    n = km.shape[0]
    t = jnp.concatenate([km] * heads, axis=0)
    r = lax.broadcasted_iota(jnp.int32, t.shape, 0)
    c = lax.broadcasted_iota(jnp.int32, t.shape, 1)
    return jnp.where((r // n) == (c // HEAD_DIM), t, jnp.zeros_like(t))


def _gate_kernel(tq, q_ref, km_ref, bias_ref):
    t = pl.program_id(1)
    nb = km_ref.shape[1]
    km_hi, km_lo = _split_bf16(_block_diag_rows(km_ref[0], HEADS))
    q_hi, q_lo = _split_bf16(q_ref[0])
    g = _dot_nt(km_hi, q_hi) + (_dot_nt(km_hi, q_lo) + _dot_nt(km_lo, q_hi))
    g = g.reshape(HEADS, nb, tq)
    j = lax.broadcasted_iota(jnp.int32, g.shape, 1)
    pos = t * tq + lax.broadcasted_iota(jnp.int32, g.shape, 2)
    bias_ref[0] = _top_blocks_bias(g, j < pos // BLOCK, 1)


def _gate(q3, km3):
    b, s, _ = q3.shape
    nb = km3.shape[1]
    tq = min(512, s)
    return pl.pallas_call(
        functools.partial(_gate_kernel, tq),
        out_shape=jax.ShapeDtypeStruct((b, HEADS, nb, s), F32),
        grid=(b, s // tq),
        in_specs=[pl.BlockSpec((1, tq, WB), lambda bi, t: (bi, t, 0)),
                  pl.BlockSpec((1, nb, WB), lambda bi, t: (bi, 0, 0))],
        out_specs=pl.BlockSpec((1, HEADS, nb, tq), lambda bi, t: (bi, 0, 0, t)),
        compiler_params=_params(2),
        name="moba_gate",
    )(q3, km3)


def _attend_kernel(q_ref, k_ref, vt_ref, bias_ref, o_ref, acc_ref):
    i = pl.program_id(2)
    hpl = LANES // HEAD_DIM
    qb = (q_ref[0] * (HEAD_DIM ** -0.5)).astype(BF16)
    lane = lax.broadcasted_iota(jnp.int32, qb.shape, 1)
    qh = [jnp.where((lane // HEAD_DIM) == h, qb, jnp.zeros_like(qb)) for h in range(hpl)]

    def block(j, h, bias):
        start = pl.multiple_of(j * BLOCK, BLOCK)
        s = _dot_nt(k_ref[0, pl.ds(start, BLOCK), :], qh[h])
        return s + bias

    def pv(j, h, p):
        return _dot(vt_ref[j, 0, h * HEAD_DIM:(h + 1) * HEAD_DIM, :], p.astype(BF16))

    kk = lax.broadcasted_iota(jnp.int32, (BLOCK, BLOCK), 0)
    qq = lax.broadcasted_iota(jnp.int32, (BLOCK, BLOCK), 1)
    causal = jnp.where(kk <= qq, 0.0, NEG_INF).astype(F32)
    m0, l0 = [], []
    for h in range(hpl):
        s = block(i, h, causal)
        m = jnp.max(s, axis=0, keepdims=True)
        p = jnp.exp(s - m)
        m0.append(m)
        l0.append(jnp.sum(p, axis=0, keepdims=True))
        acc_ref[h] = pv(i, h, p)

    def body(j, carry):
        ms, ls = carry
        new_m, new_l = [], []
        for h in range(hpl):
            s = block(j, h, bias_ref[0, h, pl.ds(j, 1), :])
            m = jnp.maximum(ms[h], jnp.max(s, axis=0, keepdims=True))
            alpha = jnp.exp(ms[h] - m)
            p = jnp.exp(s - m)
            new_m.append(m)
            new_l.append(alpha * ls[h] + jnp.sum(p, axis=0, keepdims=True))
            acc_ref[h] = alpha * acc_ref[h] + pv(j, h, p)
        return tuple(new_m), tuple(new_l)

    _, ls = lax.fori_loop(0, i, body, (tuple(m0), tuple(l0)))
    o = jnp.concatenate([acc_ref[h] / ls[h] for h in range(hpl)], axis=0)
    o_ref[0] = o.T


def _attend(q3, kbf3, vt4, bias):
    b, s, _ = q3.shape
    nb = s // BLOCK
    hpl = LANES // HEAD_DIM
    return pl.pallas_call(
        _attend_kernel,
        out_shape=jax.ShapeDtypeStruct((b, s, WB), F32),
        grid=(b, WB // LANES, nb),
        in_specs=[pl.BlockSpec((1, BLOCK, LANES), lambda bi, hp, i: (bi, i, hp)),
                  pl.BlockSpec((1, s, LANES), lambda bi, hp, i: (bi, 0, hp)),
                  pl.BlockSpec((nb, 1, LANES, BLOCK), lambda bi, hp, i: (bi, 0, hp, 0)),
                  pl.BlockSpec((1, hpl, nb, BLOCK), lambda bi, hp, i: (bi, hp, 0, i))],
        out_specs=pl.BlockSpec((1, BLOCK, LANES), lambda bi, hp, i: (bi, i, hp)),
        scratch_shapes=[pltpu.VMEM((hpl, HEAD_DIM, BLOCK), F32)],
        compiler_params=_params(3),
        name="moba_attend",
    )(q3, kbf3, vt4, bias)


def _sample_attend_kernel(n_pages, pt_ref, q_ref, kn_ref, vn_ref, *rest):
    del pt_ref
    k_pages = rest[:n_pages]
    v_pages = rest[n_pages:2 * n_pages]
    o_ref = rest[2 * n_pages]
    t_len = q_ref.shape[0]
    ppb = BLOCK // k_pages[0].shape[2]
    nf = n_pages // ppb
    q = q_ref[...]
    qbd = _block_diag_rows(q, HEADS)
    sums = [jnp.sum(kp[0, 0], axis=0, keepdims=True) for kp in k_pages]
    km = jnp.concatenate([sum(sums[jb * ppb:(jb + 1) * ppb]) for jb in range(nf)], axis=0) * (1.0 / BLOCK)
    q_hi, q_lo = _split_bf16(qbd)
    km_hi, km_lo = _split_bf16(km)
    g = _dot_nt(q_hi, km_hi) + (_dot_nt(q_lo, km_hi) + _dot_nt(q_hi, km_lo))
    bias = _top_blocks_bias(g, jnp.ones(g.shape, jnp.bool_), 1)

    qs = (qbd * (HEAD_DIM ** -0.5)).astype(BF16)
    s_pages = [_dot_nt(qs, kp[0, 0].astype(BF16)) + bias[:, p // ppb:p // ppb + 1]
               for p, kp in enumerate(k_pages)]
    r = lax.broadcasted_iota(jnp.int32, (HEADS * t_len, t_len), 0)
    c = lax.broadcasted_iota(jnp.int32, (HEADS * t_len, t_len), 1)
    s_own = jnp.where(c <= r % t_len, _dot_nt(qs, kn_ref[...].astype(BF16)), NEG_INF)
    m = jnp.max(s_own, axis=1, keepdims=True)
    for s in s_pages:
        m = jnp.maximum(m, jnp.max(s, axis=1, keepdims=True))
    p_own = jnp.exp(s_own - m)
    l = jnp.sum(p_own, axis=1, keepdims=True)
    vn = vn_ref[...]
    acc = jnp.zeros((HEADS * t_len, WB), F32)
    for t in range(t_len):
        acc = acc + p_own[:, t:t + 1] * vn[t:t + 1, :]
    for s, vp in zip(s_pages, v_pages):
        p = jnp.exp(s - m)
        l = l + jnp.sum(p, axis=1, keepdims=True)
        acc = acc + _dot(p.astype(BF16), vp[0, 0].astype(BF16))
    out = acc / l
    rr = lax.broadcasted_iota(jnp.int32, out.shape, 0)
    cc = lax.broadcasted_iota(jnp.int32, out.shape, 1)
    out = jnp.where((rr // t_len) == (cc // HEAD_DIM), out, 0.0)
    o_ref[...] = jnp.sum(out.reshape(HEADS, t_len, WB), axis=0)


def _sample_attend(layer, q2, kn2, vn2, cache_k4, cache_v4, page_table, t_len):
    db, n_pages = page_table.shape
    page = cache_k4.shape[2]
    assert BLOCK % page == 0 and n_pages % (BLOCK // page) == 0, "past length must be whole MoBA blocks"
    assert t_len <= BLOCK
    row = pl.BlockSpec((t_len, WB), lambda bi, pt: (bi, 0))

    def page_spec(p):
        return pl.BlockSpec((1, 1, page, WB), lambda bi, pt: (layer, pt[bi, p], 0, 0))

    pages = [page_spec(p) for p in range(n_pages)]
    return pl.pallas_call(
        functools.partial(_sample_attend_kernel, n_pages),
        out_shape=jax.ShapeDtypeStruct((db * t_len, WB), F32),
        grid_spec=pltpu.PrefetchScalarGridSpec(
            num_scalar_prefetch=1,
            grid=(db,),
            in_specs=[row, row, row] + pages + pages,
            out_specs=row,
        ),
        compiler_params=_params(1),
        name="moba_sample",
    )(page_table, q2, kn2, vn2, *([cache_k4] * n_pages), *([cache_v4] * n_pages))


def _merge_kernel(x_ref, u_ref, va_ref, za_ref, yb_ref, zb_ref, ga_ref, gb_ref,
                  ws_ref, bt_ref, wpa_ref, wpb_ref, wo_ref, o_ref, mix_ref):
    tm = x_ref.shape[0]
    gpl = LANES // (WA // A_GROUPS)
    r = lax.broadcasted_iota(jnp.int32, (CHUNK, CHUNK), 0)
    c = lax.broadcasted_iota(jnp.int32, (CHUNK, CHUNK), 1)
    lane = lax.broadcasted_iota(jnp.int32, (CHUNK, LANES), 1)
    for ch in range(tm // CHUNK):
        rows = slice(ch * CHUNK, (ch + 1) * CHUNK)
        for lg in range(WA // LANES):
            v = va_ref[rows, lg * LANES:(lg + 1) * LANES].astype(BF16)
            mixed = None
            for gi in range(gpl):
                w = ws_ref[lg * gpl + gi]
                w = jnp.where(r >= c, w, jnp.zeros_like(w))
                vg = jnp.where((lane // (LANES // gpl)) == gi, v, jnp.zeros_like(v))
                part = _dot(w, vg)
                mixed = part if mixed is None else mixed + part
            mix_ref[rows, lg * LANES:(lg + 1) * LANES] = mixed + bt_ref[:, lg * LANES:(lg + 1) * LANES]
    za = za_ref[...]
    zb = zb_ref[...]
    ya = u_ref[...] * mix_ref[...] * (za * jax.nn.sigmoid(za))
    yb = yb_ref[...] * (zb * jax.nn.sigmoid(zb))
    m = (jax.nn.sigmoid(ga_ref[...]) * _dot(ya.astype(BF16), wpa_ref[...])
         + jax.nn.sigmoid(gb_ref[...]) * _dot(yb.astype(BF16), wpb_ref[...]))
    o_ref[...] = x_ref[...] + _dot(m.astype(BF16), wo_ref[...])


def _merge(x2d, u, va, za, yb, zb, ga, gb, ws, bt, wpa, wpb, wo):
    rows = x2d.shape[0]
    tm = 256
    row = lambda w_: pl.BlockSpec((tm, w_), lambda i: (i, 0))
    full = lambda a: pl.BlockSpec(a.shape, lambda i: (0,) * a.ndim)
    return pl.pallas_call(
        _merge_kernel,
        out_shape=jax.ShapeDtypeStruct((rows, D_MODEL), F32),
        grid=(rows // tm,),
        in_specs=[row(D_MODEL), row(WA), row(WA), row(WA), row(WB), row(WB), row(D_MODEL), row(D_MODEL),
                  full(ws), full(bt), full(wpa), full(wpb), full(wo)],
        out_specs=row(D_MODEL),
        scratch_shapes=[pltpu.VMEM((tm, WA), F32)],
        compiler_params=_params(1),
        name="merge",
    )(x2d, u, va, za, yb, zb, ga, gb, ws, bt, wpa, wpb, wo)


def _rope_tables(pos):
    half = HEAD_DIM // 2
    inv = jnp.power(jnp.float32(ROPE_THETA), -jnp.arange(half, dtype=F32) * (2.0 / HEAD_DIM))
    ang = pos.astype(F32)[:, None] * inv[None, :]
    cos, sin = jnp.cos(ang), jnp.sin(ang)
    reps = LANES // HEAD_DIM
    return (jnp.tile(jnp.concatenate([cos, cos], axis=1), (1, reps)),
            jnp.tile(jnp.concatenate([-sin, sin], axis=1), (1, reps)))


def kernel(x_prompt, x_sample, cache_k, cache_v, page_table, norm_g, w_in, v_norm_g, w_s, b_s,
           q_norm_g, k_norm_g, w_pa, w_pb, w_o):
    depth = w_in.shape[0]
    b, s, _ = x_prompt.shape
    db, t_len, _ = x_sample.shape
    n_pool, page = cache_k.shape[1], cache_k.shape[2]
    past_len = page_table.shape[1] * page
    assert s % BLOCK == 0 and (db * t_len) % BLOCK == 0 and CHUNK % t_len == 0
    nb = s // BLOCK

    cos_p, sin_p = _rope_tables(jnp.arange(s))
    cos_s, sin_s = _rope_tables(past_len + jnp.arange(db * t_len) % t_len)
    hsum = (jnp.arange(WB)[:, None] // HEAD_DIM == jnp.arange(WB)[None, :] // HEAD_DIM).astype(BF16)
    cache_k4 = cache_k.reshape(depth, n_pool, page, WB)
    cache_v4 = cache_v.reshape(depth, n_pool, page, WB)
    cw = WA // A_GROUPS

    xp = x_prompt.reshape(b * s, D_MODEL)
    xs = x_sample.reshape(db * t_len, D_MODEL)
    kp_l, vp_l, cp_l, ks_l, vs_l, cs_l = [], [], [], [], [], []
    for l in range(depth):
        ng = norm_g[l][None, :]
        vg = v_norm_g[l][None, :]
        qg = jnp.tile(q_norm_g[l], HEADS)[None, :]
        kg = jnp.tile(k_norm_g[l], HEADS)[None, :]
        w = w_in[l].astype(BF16)
        wpa, wpb, wo = w_pa[l].astype(BF16), w_pb[l].astype(BF16), w_o[l].astype(BF16)
        ws_p = w_s[l].astype(BF16)
        bt_p = jnp.repeat(b_s[l].T, cw, axis=1)
        eye = jnp.eye(CHUNK // t_len, dtype=F32)
        ws_s = jnp.stack([jnp.kron(eye, w_s[l, g, :t_len, :t_len]) for g in range(A_GROUPS)]).astype(BF16)
        bt_s = jnp.tile(bt_p[:t_len], (CHUNK // t_len, 1))

        u, va, za, q, k, vb, zb, ga, gb, kbf, vt, km = _proj(True, xp, s, ng, w, vg, qg, kg, cos_p, sin_p, hsum)
        bias = _gate(q.reshape(b, s, WB), km.reshape(b, nb, WB))
        yb = _attend(q.reshape(b, s, WB), kbf.reshape(b, s, WB), vt, bias).reshape(b * s, WB)
        xp = _merge(xp, u, va, za, yb, zb, ga, gb, ws_p, bt_p, wpa, wpb, wo)
        kp_l.append(k.reshape(b, s, HEADS, HEAD_DIM))
        vp_l.append(vb.reshape(b, s, HEADS, HEAD_DIM))
        cp_l.append(va.reshape(b, s, WA)[:, s - CHUNK:])

        u, va, za, q, k, vb, zb, ga, gb = _proj(False, xs, t_len, ng, w, vg, qg, kg, cos_s, sin_s, hsum)
        yb = _sample_attend(l, q, k, vb, cache_k4, cache_v4, page_table, t_len)
        xs = _merge(xs, u, va, za, yb, zb, ga, gb, ws_s, bt_s, wpa, wpb, wo)
        ks_l.append(k.reshape(db, t_len, HEADS, HEAD_DIM))
        vs_l.append(vb.reshape(db, t_len, HEADS, HEAD_DIM))
        cs_l.append(va.reshape(db, t_len, WA))

    return (xp.reshape(b, s, D_MODEL), xs.reshape(db, t_len, D_MODEL),
            jnp.stack(kp_l), jnp.stack(vp_l), jnp.stack(cp_l),
            jnp.stack(ks_l), jnp.stack(vs_l), jnp.stack(cs_l))
```

```python
import functools
import math

import jax
import jax.numpy as jnp
import numpy as np
from jax import lax
from jax.experimental import pallas as pl
from jax.experimental.pallas import tpu as pltpu

D_MODEL = 1024
WA = 512
A_GROUPS = 8
CHUNK = 128
HEADS = 8
HEAD_DIM = 64
WB = HEADS * HEAD_DIM
BLOCK = 256
TOPK = 3
ROPE_THETA = 10000.0
EPS = 1e-6
OFF = np.cumsum((0, WA, WA, WA, WB, WB, WB, WB, D_MODEL)).tolist()

LANES = 128
AUG = HEADS * LANES
VMEM_LIMIT = 56 * 1024 * 1024
ATT_HEADS = 4
BF16_SUBLANES = 16
V_ROWS = HEAD_DIM + BF16_SUBLANES

F32 = jnp.float32
BF16 = jnp.bfloat16
NEG_INF = float("-inf")
MASKED = -1e30
QK_SCALE = HEAD_DIM ** -0.5 * math.log2(math.e)


def _dot(a, b):
    return jnp.dot(a, b, preferred_element_type=F32)


def _dot_nt(a, b):
    return lax.dot_general(a, b, (((1,), (1,)), ((), ())), preferred_element_type=F32)


def _split_bf16(x):
    hi = x.astype(BF16)
    lo = (x - hi.astype(F32)).astype(BF16)
    return hi, lo


def _params(n_axes):
    return pltpu.CompilerParams(dimension_semantics=("arbitrary",) * n_axes, vmem_limit_bytes=VMEM_LIMIT)


def _head_norm_rope(p, gain, hsum, cos, sin, first_half):
    ms = _dot((p * p).astype(BF16), hsum) * (1.0 / HEAD_DIM)
    xn = p * lax.rsqrt(ms + EPS) * gain
    partner = jnp.where(first_half, pltpu.roll(xn, WB - HEAD_DIM // 2, 1), pltpu.roll(xn, HEAD_DIM // 2, 1))
    return xn * cos + partner * sin


def _proj_kernel(prompt, nb, x_ref, ng_ref, w_ref, vg_ref, qg_ref, kg_ref, cos_ref, sin_ref, hsum_ref, *rest):
    if prompt:
        spread_ref = rest[0]
        u_ref, va_ref, za_ref, q_ref, k_ref, vb_ref, zb_ref, ga_ref, gb_ref, kaug_ref, vt_ref, km_ref = rest[1:]
    else:
        u_ref, va_ref, za_ref, q_ref, k_ref, vb_ref, zb_ref, ga_ref, gb_ref = rest
    x = x_ref[...]
    ms = jnp.mean(x * x, axis=-1, keepdims=True)
    h = (x * lax.rsqrt(ms + EPS) * ng_ref[...]).astype(BF16)

    def col(i, width):
        return _dot(h, w_ref[:, OFF[i]:OFF[i] + width])

    u_ref[...] = col(0, WA)
    va = col(1, WA)
    va_ref[...] = va * lax.rsqrt(jnp.mean(va * va, axis=-1, keepdims=True) + EPS) * vg_ref[...]
    za_ref[...] = col(2, WA)

    reps = WB // LANES
    cos = jnp.concatenate([cos_ref[...]] * reps, axis=1)
    sin = jnp.concatenate([sin_ref[...]] * reps, axis=1)
    lane = lax.broadcasted_iota(jnp.int32, (1, WB), 1)
    first_half = (lane % HEAD_DIM) < (HEAD_DIM // 2)
    hsum = hsum_ref[...]
    q_ref[...] = _head_norm_rope(col(3, WB), qg_ref[...], hsum, cos, sin, first_half)
    k = _head_norm_rope(col(4, WB), kg_ref[...], hsum, cos, sin, first_half)
    k_ref[...] = k
    vb = col(5, WB)
    vb_ref[...] = vb
    zb_ref[...] = col(6, WB)
    ga_ref[...] = col(7, D_MODEL)
    gb_ref[...] = col(8, D_MODEL)
    if prompt:
        j = pl.program_id(0) % nb
        lane_a = lax.broadcasted_iota(jnp.int32, (1, AUG), 1)
        mark = jnp.where(lane_a % LANES == HEAD_DIM + j, 1.0, 0.0)
        kaug_ref[...] = (_dot(k.astype(BF16), spread_ref[...]) + mark).astype(BF16)
        vt = vb.T.astype(BF16)
        tail = lax.broadcasted_iota(jnp.int32, (BF16_SUBLANES, vt.shape[1]), 0)
        ones_row = jnp.where(tail == 0, 1.0, 0.0).astype(BF16)
        parts = []
        for hd in range(HEADS):
            parts += [vt[hd * HEAD_DIM:(hd + 1) * HEAD_DIM], ones_row]
        vt_ref[0, 0] = jnp.concatenate(parts, axis=0)
        km_ref[0] = jnp.mean(k, axis=0, keepdims=True)


def _proj(prompt, x2d, seq, ng, w, vg, qg, kg, cos, sin, hsum, spread=None):
    rows = x2d.shape[0]
    tm = BLOCK
    nt = rows // tm
    tiles_per_seq = seq // tm if prompt else nt
    row = lambda w_: pl.BlockSpec((tm, w_), lambda i: (i, 0))
    full = lambda a: pl.BlockSpec(a.shape, lambda i: (0,) * a.ndim)
    tab = pl.BlockSpec((tm, LANES), lambda i: (i % tiles_per_seq, 0))
    out_shape = [jax.ShapeDtypeStruct((rows, w_), F32) for w_ in (WA, WA, WA, WB, WB, WB, WB, D_MODEL, D_MODEL)]
    out_specs = [row(w_) for w_ in (WA, WA, WA, WB, WB, WB, WB, D_MODEL, D_MODEL)]
    args = [x2d, ng, w, vg, qg, kg, cos, sin, hsum]
    in_specs = [row(D_MODEL), full(ng), full(w), full(vg), full(qg), full(kg), tab, tab, full(hsum)]
    if prompt:
        args.append(spread)
        in_specs.append(full(spread))
        out_shape += [jax.ShapeDtypeStruct((rows, AUG), BF16),
                      jax.ShapeDtypeStruct((nt, 1, HEADS * V_ROWS, tm), BF16),
                      jax.ShapeDtypeStruct((nt, 1, WB), F32)]
        out_specs += [row(AUG),
                      pl.BlockSpec((1, 1, HEADS * V_ROWS, tm), lambda i: (i, 0, 0, 0)),
                      pl.BlockSpec((1, 1, WB), lambda i: (i, 0, 0))]
    return pl.pallas_call(
        functools.partial(_proj_kernel, prompt, tiles_per_seq),
        out_shape=out_shape,
        grid=(nt,),
        in_specs=in_specs,
        out_specs=out_specs,
        compiler_params=_params(1),
        name="proj_prompt" if prompt else "proj_sample",
    )(*args)


def _top_blocks_bias(g, valid, axis):
    idx = lax.broadcasted_iota(jnp.int32, g.shape, axis).astype(F32)
    n = float(g.shape[axis])
    g = jnp.where(valid, g, NEG_INF)
    sel = jnp.zeros(g.shape, jnp.bool_)
    for _ in range(min(TOPK, g.shape[axis])):
        m = jnp.max(g, axis=axis, keepdims=True)
        first = jnp.min(jnp.where(g == m, idx, n), axis=axis, keepdims=True)
        pick = (idx == first) & (m > NEG_INF)
        sel = sel | pick
        g = jnp.where(pick, NEG_INF, g)
    return jnp.where(sel, 0.0, MASKED).astype(F32)


def _block_diag_rows(km, heads):
    n = km.shape[0]
    t = jnp.concatenate([km] * heads, axis=0)
    r = lax.broadcasted_iota(jnp.int32, t.shape, 0)
    c = lax.broadcasted_iota(jnp.int32, t.shape, 1)
    return jnp.where((r // n) == (c // HEAD_DIM), t, jnp.zeros_like(t))


def _gate_kernel(tq, q_ref, km_ref, spread_ref, place_ref, qaug_ref):
    t = pl.program_id(1)
    nb = km_ref.shape[1]
    q = q_ref[0]
    km_hi, km_lo = _split_bf16(_block_diag_rows(km_ref[0], HEADS))
    q_hi, q_lo = _split_bf16(q)
    g = _dot_nt(km_hi, q_hi) + (_dot_nt(km_hi, q_lo) + _dot_nt(km_lo, q_hi))
    g = g.reshape(HEADS, nb, tq)
    j = lax.broadcasted_iota(jnp.int32, g.shape, 1)
    pos = t * tq + lax.broadcasted_iota(jnp.int32, g.shape, 2)
    bias = _top_blocks_bias(g, j < pos // BLOCK, 1)
    bias = jnp.where(j == pos // BLOCK, 0.0, bias).reshape(HEADS * nb, tq)
    qaug = _dot((q * QK_SCALE).astype(BF16), spread_ref[...]) + _dot(bias.T.astype(BF16), place_ref[...])
    qaug_ref[0] = qaug.astype(BF16)


def _gate(q3, km3, spread, place):
    b, s, _ = q3.shape
    nb = km3.shape[1]
    tq = min(512, s)
    full = lambda a: pl.BlockSpec(a.shape, lambda bi, t: (0,) * a.ndim)
    return pl.pallas_call(
        functools.partial(_gate_kernel, tq),
        out_shape=jax.ShapeDtypeStruct((b, s, AUG), BF16),
        grid=(b, s // tq),
        in_specs=[pl.BlockSpec((1, tq, WB), lambda bi, t: (bi, t, 0)),
                  pl.BlockSpec((1, nb, WB), lambda bi, t: (bi, 0, 0)),
                  full(spread), full(place)],
        out_specs=pl.BlockSpec((1, tq, AUG), lambda bi, t: (bi, t, 0)),
        compiler_params=_params(2),
        name="moba_gate",
    )(q3, km3, spread, place)


def _attend_kernel(q_ref, k_ref, vt_ref, o_ref, s_scr, p_scr, acc_ref, m_ref, alpha_ref, cmax_ref):
    i = pl.program_id(2)
    heads = range(ATT_HEADS)

    def scores(blk, h):
        start = pl.multiple_of(blk * BLOCK, BLOCK)
        grp = slice(h * LANES, (h + 1) * LANES)
        return _dot_nt(k_ref[0, pl.ds(start, BLOCK), grp], q_ref[0, :, grp])

    def pv(blk, h):
        return _dot(vt_ref[blk, 0, h * V_ROWS:(h + 1) * V_ROWS, :], p_scr[h])

    kk = lax.broadcasted_iota(jnp.int32, (BLOCK, BLOCK), 0)
    qq = lax.broadcasted_iota(jnp.int32, (BLOCK, BLOCK), 1)
    causal = jnp.where(kk <= qq, 0.0, MASKED).astype(F32)
    for h in heads:
        s = scores(i, h) + causal
        s_scr[h] = s
        cmax_ref[h] = jnp.max(s, axis=0, keepdims=True)
        p_scr[h] = jnp.zeros((BLOCK, BLOCK), BF16)
        acc_ref[h] = jnp.zeros((V_ROWS, BLOCK), F32)
        m_ref[h] = jnp.full((1, BLOCK), MASKED, F32)
        alpha_ref[h] = jnp.zeros((1, BLOCK), F32)

    def block_of(t):
        return jnp.where(t <= 0, i, t - 1)

    def body(t, carry):
        pvs = [pv(block_of(t - 1), h) for h in heads]
        nxt = [scores(t, h) for h in heads]
        for h in heads:
            m_old = m_ref[h]
            m_new = jnp.maximum(m_old, cmax_ref[h])
            acc_ref[h] = alpha_ref[h] * acc_ref[h] + pvs[h]
            alpha_ref[h] = jnp.exp2(m_old - m_new)
            m_ref[h] = m_new
            p_scr[h] = jnp.exp2(s_scr[h] - m_new).astype(BF16)
        for h in heads:
            s_scr[h] = nxt[h]
            cmax_ref[h] = jnp.max(nxt[h], axis=0, keepdims=True)
        return carry

    lax.fori_loop(0, i + 1, body, 0)
    outs = []
    for h in heads:
        acc = alpha_ref[h] * acc_ref[h] + pv(block_of(i), h)
        outs.append(acc[:HEAD_DIM] / acc[HEAD_DIM:HEAD_DIM + 1])
    o_ref[0] = jnp.concatenate(outs, axis=0).T


def _attend(qaug, kaug, vt4, nb):
    b, s, _ = qaug.shape
    return pl.pallas_call(
        _attend_kernel,
        out_shape=jax.ShapeDtypeStruct((b, s, WB), F32),
        grid=(b, HEADS // ATT_HEADS, nb),
        in_specs=[pl.BlockSpec((1, BLOCK, ATT_HEADS * LANES), lambda bi, hg, i: (bi, i, hg)),
                  pl.BlockSpec((1, s, ATT_HEADS * LANES), lambda bi, hg, i: (bi, 0, hg)),
                  pl.BlockSpec((nb, 1, ATT_HEADS * V_ROWS, BLOCK), lambda bi, hg, i: (bi, 0, hg, 0))],
        out_specs=pl.BlockSpec((1, BLOCK, ATT_HEADS * HEAD_DIM), lambda bi, hg, i: (bi, i, hg)),
        scratch_shapes=[pltpu.VMEM((ATT_HEADS, BLOCK, BLOCK), F32),
                        pltpu.VMEM((ATT_HEADS, BLOCK, BLOCK), BF16),
                        pltpu.VMEM((ATT_HEADS, V_ROWS, BLOCK), F32),
                        pltpu.VMEM((ATT_HEADS, 1, BLOCK), F32),
                        pltpu.VMEM((ATT_HEADS, 1, BLOCK), F32),
                        pltpu.VMEM((ATT_HEADS, 1, BLOCK), F32)],
        compiler_params=_params(3),
        name="moba_attend",
    )(qaug, kaug, vt4)


def _sample_attend_kernel(n_pages, pt_ref, q_ref, kn_ref, vn_ref, *rest):
    del pt_ref
    kt_pages = rest[:n_pages]
    vt_pages = rest[n_pages:2 * n_pages]
    o_ref = rest[2 * n_pages]
    t_len = q_ref.shape[0]
    ppb = BLOCK // kt_pages[0].shape[3]
    nf = n_pages // ppb
    qbd = _block_diag_rows(q_ref[...], HEADS)
    lane = lax.broadcasted_iota(jnp.int32, (WB, nf), 1)
    kmt = jnp.zeros((WB, nf), F32)
    for jb in range(nf):
        blk = kt_pages[jb * ppb][0, 0]
        for p in range(jb * ppb + 1, (jb + 1) * ppb):
            blk = blk + kt_pages[p][0, 0]
        kmt = jnp.where(lane == jb, jnp.sum(blk, axis=1, keepdims=True) * (1.0 / BLOCK), kmt)
    q_hi, q_lo = _split_bf16(qbd)
    km_hi, km_lo = _split_bf16(kmt)
    g = _dot(q_hi, km_hi) + (_dot(q_lo, km_hi) + _dot(q_hi, km_lo))
    bias = _top_blocks_bias(g, jnp.ones(g.shape, jnp.bool_), 1)

    qs = (qbd * QK_SCALE).astype(BF16)
    s_pages = [_dot(qs, kp[0, 0].astype(BF16)) + bias[:, p // ppb:p // ppb + 1]
               for p, kp in enumerate(kt_pages)]
    r = lax.broadcasted_iota(jnp.int32, (HEADS * t_len, t_len), 0)
    c = lax.broadcasted_iota(jnp.int32, (HEADS * t_len, t_len), 1)
    s_own = jnp.where(c <= r % t_len, _dot_nt(qs, kn_ref[...].astype(BF16)), MASKED)
    m = jnp.max(s_own, axis=1, keepdims=True)
    for s in s_pages:
        m = jnp.maximum(m, jnp.max(s, axis=1, keepdims=True))
    p_own = jnp.exp2(s_own - m)
    l = jnp.sum(p_own, axis=1, keepdims=True)
    vn = vn_ref[...]
    acc = jnp.zeros((HEADS * t_len, WB), F32)
    for t in range(t_len):
        acc = acc + p_own[:, t:t + 1] * vn[t:t + 1, :]
    for s, vp in zip(s_pages, vt_pages):
        p = jnp.exp2(s - m)
        l = l + jnp.sum(p, axis=1, keepdims=True)
        acc = acc + _dot_nt(p.astype(BF16), vp[0, 0].astype(BF16))
    out = acc / l
    rr = lax.broadcasted_iota(jnp.int32, out.shape, 0)
    cc = lax.broadcasted_iota(jnp.int32, out.shape, 1)
    out = jnp.where((rr // t_len) == (cc // HEAD_DIM), out, 0.0)
    o_ref[...] = jnp.sum(out.reshape(HEADS, t_len, WB), axis=0)


def _sample_attend(layer, q2, kn2, vn2, cache_kt, cache_vt, page_table, t_len):
    db, n_pages = page_table.shape
    page = cache_kt.shape[3]
    assert BLOCK % page == 0 and n_pages % (BLOCK // page) == 0, "past length must be whole MoBA blocks"
    assert t_len <= BLOCK
    row = pl.BlockSpec((t_len, WB), lambda bi, pt: (bi, 0))

    def page_spec(p):
        return pl.BlockSpec((1, 1, WB, page), lambda bi, pt: (layer, pt[bi, p], 0, 0))

    pages = [page_spec(p) for p in range(n_pages)]
    return pl.pallas_call(
        functools.partial(_sample_attend_kernel, n_pages),
        out_shape=jax.ShapeDtypeStruct((db * t_len, WB), F32),
        grid_spec=pltpu.PrefetchScalarGridSpec(
            num_scalar_prefetch=1,
            grid=(db,),
            in_specs=[row, row, row] + pages + pages,
            out_specs=row,
        ),
        compiler_params=_params(1),
        name="moba_sample",
    )(page_table, q2, kn2, vn2, *([cache_kt] * n_pages), *([cache_vt] * n_pages))


def _merge_kernel(x_ref, u_ref, va_ref, za_ref, yb_ref, zb_ref, ga_ref, gb_ref,
                  ws_ref, bt_ref, wpa_ref, wpb_ref, wo_ref, o_ref, mix_ref):
    tm = x_ref.shape[0]
    gpl = LANES // (WA // A_GROUPS)
    r = lax.broadcasted_iota(jnp.int32, (CHUNK, CHUNK), 0)
    c = lax.broadcasted_iota(jnp.int32, (CHUNK, CHUNK), 1)
    lane = lax.broadcasted_iota(jnp.int32, (CHUNK, LANES), 1)
    for ch in range(tm // CHUNK):
        rows = slice(ch * CHUNK, (ch + 1) * CHUNK)
        for lg in range(WA // LANES):
            v = va_ref[rows, lg * LANES:(lg + 1) * LANES].astype(BF16)
            mixed = None
            for gi in range(gpl):
                w = ws_ref[lg * gpl + gi]
                w = jnp.where(r >= c, w, jnp.zeros_like(w))
                vg = jnp.where((lane // (LANES // gpl)) == gi, v, jnp.zeros_like(v))
                part = _dot(w, vg)
                mixed = part if mixed is None else mixed + part
            mix_ref[rows, lg * LANES:(lg + 1) * LANES] = mixed + bt_ref[:, lg * LANES:(lg + 1) * LANES]
    za = za_ref[...]
    zb = zb_ref[...]
    ya = u_ref[...] * mix_ref[...] * (za * jax.nn.sigmoid(za))
    yb = yb_ref[...] * (zb * jax.nn.sigmoid(zb))
    m = (jax.nn.sigmoid(ga_ref[...]) * _dot(ya.astype(BF16), wpa_ref[...])
         + jax.nn.sigmoid(gb_ref[...]) * _dot(yb.astype(BF16), wpb_ref[...]))
    o_ref[...] = x_ref[...] + _dot(m.astype(BF16), wo_ref[...])


def _merge(x2d, u, va, za, yb, zb, ga, gb, ws, bt, wpa, wpb, wo):
    rows = x2d.shape[0]
    tm = 256
    row = lambda w_: pl.BlockSpec((tm, w_), lambda i: (i, 0))
    full = lambda a: pl.BlockSpec(a.shape, lambda i: (0,) * a.ndim)
    return pl.pallas_call(
        _merge_kernel,
        out_shape=jax.ShapeDtypeStruct((rows, D_MODEL), F32),
        grid=(rows // tm,),
        in_specs=[row(D_MODEL), row(WA), row(WA), row(WA), row(WB), row(WB), row(D_MODEL), row(D_MODEL),
                  full(ws), full(bt), full(wpa), full(wpb), full(wo)],
        out_specs=row(D_MODEL),
        scratch_shapes=[pltpu.VMEM((tm, WA), F32)],
        compiler_params=_params(1),
        name="merge",
    )(x2d, u, va, za, yb, zb, ga, gb, ws, bt, wpa, wpb, wo)


def _rope_tables(pos):
    half = HEAD_DIM // 2
    inv = jnp.power(jnp.float32(ROPE_THETA), -jnp.arange(half, dtype=F32) * (2.0 / HEAD_DIM))
    ang = pos.astype(F32)[:, None] * inv[None, :]
    cos, sin = jnp.cos(ang), jnp.sin(ang)
    reps = LANES // HEAD_DIM
    return (jnp.tile(jnp.concatenate([cos, cos], axis=1), (1, reps)),
            jnp.tile(jnp.concatenate([-sin, sin], axis=1), (1, reps)))


def _spread_matrix():
    m = np.zeros((WB, AUG), np.float32)
    c = np.arange(WB)
    m[c, (c // HEAD_DIM) * LANES + c % HEAD_DIM] = 1.0
    return jnp.asarray(m, BF16)


def _place_matrix(nb):
    m = np.zeros((HEADS * nb, AUG), np.float32)
    r = np.arange(HEADS * nb)
    m[r, (r // nb) * LANES + HEAD_DIM + r % nb] = 1.0
    return jnp.asarray(m, BF16)


def kernel(x_prompt, x_sample, cache_k, cache_v, page_table, norm_g, w_in, v_norm_g, w_s, b_s,
           q_norm_g, k_norm_g, w_pa, w_pb, w_o):
    depth = w_in.shape[0]
    b, s, _ = x_prompt.shape
    db, t_len, _ = x_sample.shape
    n_pool, page = cache_k.shape[1], cache_k.shape[2]
    past_len = page_table.shape[1] * page
    assert s % BLOCK == 0 and (db * t_len) % BLOCK == 0 and CHUNK % t_len == 0
    nb = s // BLOCK
    assert nb <= LANES - HEAD_DIM, "one spare lane per key block"

    cos_p, sin_p = _rope_tables(jnp.arange(s))
    cos_s, sin_s = _rope_tables(past_len + jnp.arange(db * t_len) % t_len)
    hsum = (jnp.arange(WB)[:, None] // HEAD_DIM == jnp.arange(WB)[None, :] // HEAD_DIM).astype(BF16)
    spread = _spread_matrix()
    place = _place_matrix(nb)
    cache_kt = cache_k.transpose(0, 1, 3, 4, 2).reshape(depth, n_pool, WB, page)
    cache_vt = cache_v.transpose(0, 1, 3, 4, 2).reshape(depth, n_pool, WB, page)
    cw = WA // A_GROUPS

    xp = x_prompt.reshape(b * s, D_MODEL)
    xs = x_sample.reshape(db * t_len, D_MODEL)
    kp_l, vp_l, cp_l, ks_l, vs_l, cs_l = [], [], [], [], [], []
    for l in range(depth):
        ng = norm_g[l][None, :]
        vg = v_norm_g[l][None, :]
        qg = jnp.tile(q_norm_g[l], HEADS)[None, :]
        kg = jnp.tile(k_norm_g[l], HEADS)[None, :]
        w = w_in[l].astype(BF16)
        wpa, wpb, wo = w_pa[l].astype(BF16), w_pb[l].astype(BF16), w_o[l].astype(BF16)
        ws_p = w_s[l].astype(BF16)
        bt_p = jnp.repeat(b_s[l].T, cw, axis=1)
        eye = jnp.eye(CHUNK // t_len, dtype=F32)
        ws_s = jnp.stack([jnp.kron(eye, w_s[l, g, :t_len, :t_len]) for g in range(A_GROUPS)]).astype(BF16)
        bt_s = jnp.tile(bt_p[:t_len], (CHUNK // t_len, 1))

        u, va, za, q, k, vb, zb, ga, gb, kaug, vt, km = _proj(
            True, xp, s, ng, w, vg, qg, kg, cos_p, sin_p, hsum, spread)
        qaug = _gate(q.reshape(b, s, WB), km.reshape(b, nb, WB), spread, place)
        yb = _attend(qaug, kaug.reshape(b, s, AUG), vt, nb).reshape(b * s, WB)
        xp = _merge(xp, u, va, za, yb, zb, ga, gb, ws_p, bt_p, wpa, wpb, wo)
        kp_l.append(k.reshape(b, s, HEADS, HEAD_DIM))
        vp_l.append(vb.reshape(b, s, HEADS, HEAD_DIM))
        cp_l.append(va.reshape(b, s, WA)[:, s - CHUNK:])

        u, va, za, q, k, vb, zb, ga, gb = _proj(False, xs, t_len, ng, w, vg, qg, kg, cos_s, sin_s, hsum)
        yb = _sample_attend(l, q, k, vb, cache_kt, cache_vt, page_table, t_len)
        xs = _merge(xs, u, va, za, yb, zb, ga, gb, ws_s, bt_s, wpa, wpb, wo)
        ks_l.append(k.reshape(db, t_len, HEADS, HEAD_DIM))
        vs_l.append(vb.reshape(db, t_len, HEADS, HEAD_DIM))
        cs_l.append(va.reshape(db, t_len, WA))

    return (xp.reshape(b, s, D_MODEL), xs.reshape(db, t_len, D_MODEL),
            jnp.stack(kp_l), jnp.stack(vp_l), jnp.stack(cp_l),
            jnp.stack(ks_l), jnp.stack(vs_l), jnp.stack(cs_l))
```

```python
import functools
import math

import jax
import jax.numpy as jnp
import numpy as np
from jax import lax
from jax.experimental import pallas as pl
from jax.experimental.pallas import tpu as pltpu

D_MODEL = 1024
WA = 512
A_GROUPS = 8
CHUNK = 128
HEADS = 8
HEAD_DIM = 64
WB = HEADS * HEAD_DIM
BLOCK = 256
TOPK = 3
ROPE_THETA = 10000.0
EPS = 1e-6
OFF = np.cumsum((0, WA, WA, WA, WB, WB, WB, WB, D_MODEL)).tolist()

LANES = 128
AUG = HEADS * LANES
VMEM_LIMIT = 56 * 1024 * 1024
ATT_HEADS = 4
BF16_SUBLANES = 16
V_ROWS = HEAD_DIM + BF16_SUBLANES

F32 = jnp.float32
BF16 = jnp.bfloat16
NEG_INF = float("-inf")
MASKED = -1e30
QK_SCALE = HEAD_DIM ** -0.5 * math.log2(math.e)


def _dot(a, b):
    return jnp.dot(a, b, preferred_element_type=F32)


def _dot_nt(a, b):
    return lax.dot_general(a, b, (((1,), (1,)), ((), ())), preferred_element_type=F32)


def _split_bf16(x):
    hi = x.astype(BF16)
    lo = (x - hi.astype(F32)).astype(BF16)
    return hi, lo


def _params(n_axes):
    return pltpu.CompilerParams(dimension_semantics=("arbitrary",) * n_axes, vmem_limit_bytes=VMEM_LIMIT)


def _head_norm_rope(p, gain, hsum, cos, sin, first_half):
    ms = _dot((p * p).astype(BF16), hsum) * (1.0 / HEAD_DIM)
    xn = p * lax.rsqrt(ms + EPS) * gain
    partner = jnp.where(first_half, pltpu.roll(xn, WB - HEAD_DIM // 2, 1), pltpu.roll(xn, HEAD_DIM // 2, 1))
    return xn * cos + partner * sin


def _proj_kernel(prompt, nb, x_ref, ng_ref, w_ref, vg_ref, qg_ref, kg_ref, cos_ref, sin_ref, hsum_ref, *rest):
    if prompt:
        spread_ref = rest[0]
        (u_ref, va_ref, za_ref, q_ref, zb_ref, ga_ref, gb_ref,
         kaug_ref, vt_ref, km_ref, kt_ref, vtf_ref) = rest[-12:]
    else:
        u_ref, va_ref, za_ref, q_ref, zb_ref, ga_ref, gb_ref, k_ref, vb_ref = rest
    x = x_ref[...]
    ms = jnp.mean(x * x, axis=-1, keepdims=True)
    h = (x * lax.rsqrt(ms + EPS) * ng_ref[...]).astype(BF16)

    def col(i, width):
        return _dot(h, w_ref[:, OFF[i]:OFF[i] + width])

    u_ref[...] = col(0, WA).astype(u_ref.dtype)
    va = col(1, WA)
    va_ref[...] = va * lax.rsqrt(jnp.mean(va * va, axis=-1, keepdims=True) + EPS) * vg_ref[...]
    za_ref[...] = col(2, WA).astype(za_ref.dtype)

    reps = WB // LANES
    cos = jnp.concatenate([cos_ref[...]] * reps, axis=1)
    sin = jnp.concatenate([sin_ref[...]] * reps, axis=1)
    lane = lax.broadcasted_iota(jnp.int32, (1, WB), 1)
    first_half = (lane % HEAD_DIM) < (HEAD_DIM // 2)
    hsum = hsum_ref[...]
    q_ref[...] = _head_norm_rope(col(3, WB), qg_ref[...], hsum, cos, sin, first_half)
    k = _head_norm_rope(col(4, WB), kg_ref[...], hsum, cos, sin, first_half)
    vb = col(5, WB)
    zb_ref[...] = col(6, WB).astype(zb_ref.dtype)
    ga_ref[...] = col(7, D_MODEL).astype(ga_ref.dtype)
    gb_ref[...] = col(8, D_MODEL).astype(gb_ref.dtype)
    if not prompt:
        k_ref[...] = k
        vb_ref[...] = vb
    else:
        j = pl.program_id(0) % nb
        lane_a = lax.broadcasted_iota(jnp.int32, (1, AUG), 1)
        mark = jnp.where(lane_a % LANES == HEAD_DIM + j, 1.0, 0.0)
        kaug_ref[...] = (_dot(k.astype(BF16), spread_ref[...]) + mark).astype(BF16)
        kt_ref[0, 0] = k.T
        vtf = vb.T
        vtf_ref[0, 0] = vtf
        vt = vtf.astype(BF16)
        tail = lax.broadcasted_iota(jnp.int32, (BF16_SUBLANES, vt.shape[1]), 0)
        ones_row = jnp.where(tail == 0, 1.0, 0.0).astype(BF16)
        parts = []
        for hd in range(HEADS):
            parts += [vt[hd * HEAD_DIM:(hd + 1) * HEAD_DIM], ones_row]
        vt_ref[0, 0] = jnp.concatenate(parts, axis=0)
        km_ref[0] = jnp.mean(k, axis=0, keepdims=True)


def _proj(prompt, x2d, seq, ng, w, vg, qg, kg, cos, sin, hsum, spread=None, layer=0, depth=1, kv_prev=None):
    rows = x2d.shape[0]
    tm = BLOCK
    nt = rows // tm
    tiles_per_seq = seq // tm if prompt else nt
    row = lambda w_: pl.BlockSpec((tm, w_), lambda i: (i, 0))
    full = lambda a: pl.BlockSpec(a.shape, lambda i: (0,) * a.ndim)
    tab = pl.BlockSpec((tm, LANES), lambda i: (i % tiles_per_seq, 0))
    widths = (WA, WA, WA, WB, WB, D_MODEL, D_MODEL)
    dtypes = (BF16, F32, BF16, F32, BF16, BF16, BF16)
    out_shape = [jax.ShapeDtypeStruct((rows, w_), d_) for w_, d_ in zip(widths, dtypes)]
    out_specs = [row(w_) for w_ in widths]
    args = [x2d, ng, w, vg, qg, kg, cos, sin, hsum]
    in_specs = [row(D_MODEL), full(ng), full(w), full(vg), full(qg), full(kg), tab, tab, full(hsum)]
    aliases = {}
    if prompt:
        args.append(spread)
        in_specs.append(full(spread))
        batch = rows // seq
        kv_shape = jax.ShapeDtypeStruct((depth, batch, WB, seq), F32)
        kv_spec = pl.BlockSpec((1, 1, WB, tm), lambda i: (layer, i // tiles_per_seq, 0, i % tiles_per_seq))
        out_shape += [jax.ShapeDtypeStruct((rows, AUG), BF16),
                      jax.ShapeDtypeStruct((nt, 1, HEADS * V_ROWS, tm), BF16),
                      jax.ShapeDtypeStruct((nt, 1, WB), F32),
                      kv_shape, kv_shape]
        out_specs += [row(AUG),
                      pl.BlockSpec((1, 1, HEADS * V_ROWS, tm), lambda i: (i, 0, 0, 0)),
                      pl.BlockSpec((1, 1, WB), lambda i: (i, 0, 0)),
                      kv_spec, kv_spec]
        aliases = {len(args): len(out_shape) - 2, len(args) + 1: len(out_shape) - 1}
        args += list(kv_prev)
        in_specs += [pl.BlockSpec(memory_space=pl.ANY)] * 2
    else:
        out_shape += [jax.ShapeDtypeStruct((rows, WB), F32)] * 2
        out_specs += [row(WB)] * 2
    return pl.pallas_call(
        functools.partial(_proj_kernel, prompt, tiles_per_seq),
        out_shape=out_shape,
        grid=(nt,),
        in_specs=in_specs,
        out_specs=out_specs,
        input_output_aliases=aliases,
        compiler_params=_params(1),
        name="proj_prompt" if prompt else "proj_sample",
    )(*args)


def _top_blocks_bias(g, valid, axis):
    idx = lax.broadcasted_iota(jnp.int32, g.shape, axis).astype(F32)
    n = float(g.shape[axis])
    g = jnp.where(valid, g, NEG_INF)
    sel = jnp.zeros(g.shape, jnp.bool_)
    for _ in range(min(TOPK, g.shape[axis])):
        m = jnp.max(g, axis=axis, keepdims=True)
        first = jnp.min(jnp.where(g == m, idx, n), axis=axis, keepdims=True)
        pick = (idx == first) & (m > NEG_INF)
        sel = sel | pick
        g = jnp.where(pick, NEG_INF, g)
    return jnp.where(sel, 0.0, MASKED).astype(F32)


def _block_diag_rows(km, heads):
    n = km.shape[0]
    t = jnp.concatenate([km] * heads, axis=0)
    r = lax.broadcasted_iota(jnp.int32, t.shape, 0)
    c = lax.broadcasted_iota(jnp.int32, t.shape, 1)
    return jnp.where((r // n) == (c // HEAD_DIM), t, jnp.zeros_like(t))


def _gate_kernel(tq, q_ref, km_ref, spread_ref, place_ref, qaug_ref):
    t = pl.program_id(1)
    nb = km_ref.shape[1]
    q = q_ref[0]
    km_hi, km_lo = _split_bf16(_block_diag_rows(km_ref[0], HEADS))
    q_hi, q_lo = _split_bf16(q)
    g = _dot_nt(km_hi, q_hi) + (_dot_nt(km_hi, q_lo) + _dot_nt(km_lo, q_hi))
    g = g.reshape(HEADS, nb, tq)
    j = lax.broadcasted_iota(jnp.int32, g.shape, 1)
    pos = t * tq + lax.broadcasted_iota(jnp.int32, g.shape, 2)
    bias = _top_blocks_bias(g, j < pos // BLOCK, 1)
    bias = jnp.where(j == pos // BLOCK, 0.0, bias).reshape(HEADS * nb, tq)
    qaug = _dot((q * QK_SCALE).astype(BF16), spread_ref[...]) + _dot(bias.T.astype(BF16), place_ref[...])
    qaug_ref[0] = qaug.astype(BF16)


def _gate(q3, km3, spread, place):
    b, s, _ = q3.shape
    nb = km3.shape[1]
    tq = min(512, s)
    full = lambda a: pl.BlockSpec(a.shape, lambda bi, t: (0,) * a.ndim)
    return pl.pallas_call(
        functools.partial(_gate_kernel, tq),
        out_shape=jax.ShapeDtypeStruct((b, s, AUG), BF16),
        grid=(b, s // tq),
        in_specs=[pl.BlockSpec((1, tq, WB), lambda bi, t: (bi, t, 0)),
                  pl.BlockSpec((1, nb, WB), lambda bi, t: (bi, 0, 0)),
                  full(spread), full(place)],
        out_specs=pl.BlockSpec((1, tq, AUG), lambda bi, t: (bi, t, 0)),
        compiler_params=_params(2),
        name="moba_gate",
    )(q3, km3, spread, place)


def _attend_kernel(q_ref, k_ref, vt_ref, o_ref, s_scr, p_scr, acc_ref, m_ref, alpha_ref, cmax_ref):
    i = pl.program_id(2)
    heads = range(ATT_HEADS)

    def scores(blk, h):
        start = pl.multiple_of(blk * BLOCK, BLOCK)
        grp = slice(h * LANES, (h + 1) * LANES)
        return _dot_nt(k_ref[0, pl.ds(start, BLOCK), grp], q_ref[0, :, grp])

    def pv(blk, h):
        return _dot(vt_ref[blk, 0, h * V_ROWS:(h + 1) * V_ROWS, :], p_scr[h])

    kk = lax.broadcasted_iota(jnp.int32, (BLOCK, BLOCK), 0)
    qq = lax.broadcasted_iota(jnp.int32, (BLOCK, BLOCK), 1)
    causal = jnp.where(kk <= qq, 0.0, MASKED).astype(F32)
    for h in heads:
        s = scores(i, h) + causal
        s_scr[0, h] = s
        cmax_ref[0, h] = jnp.max(s, axis=0, keepdims=True)
        p_scr[h] = jnp.zeros((BLOCK, BLOCK), BF16)
        acc_ref[h] = jnp.zeros((V_ROWS, BLOCK), F32)
        m_ref[h] = jnp.full((1, BLOCK), MASKED, F32)
        alpha_ref[h] = jnp.zeros((1, BLOCK), F32)

    def block_of(t):
        return jnp.where(t <= 0, i, t - 1)

    def step(t, cur, nxt):
        pvs = [pv(block_of(t - 1), h) for h in heads]
        for h in heads:
            s = scores(t, h)
            s_scr[nxt, h] = s
            cmax_ref[nxt, h] = jnp.max(s, axis=0, keepdims=True)
        for h in heads:
            m_old = m_ref[h]
            m_new = jnp.maximum(m_old, cmax_ref[cur, h])
            acc_ref[h] = alpha_ref[h] * acc_ref[h] + pvs[h]
            alpha_ref[h] = jnp.exp2(m_old - m_new)
            m_ref[h] = m_new
            p_scr[h] = jnp.exp2(s_scr[cur, h] - m_new).astype(BF16)

    def pair(u, carry):
        step(2 * u, 0, 1)
        step(2 * u + 1, 1, 0)
        return carry

    n_steps = i + 1
    lax.fori_loop(0, n_steps // 2, pair, 0)

    def last(_, carry):
        step(i, 0, 1)
        return carry

    lax.fori_loop(0, n_steps % 2, last, 0)

    outs = []
    for h in heads:
        acc = alpha_ref[h] * acc_ref[h] + pv(block_of(i), h)
        outs.append(acc[:HEAD_DIM] / acc[HEAD_DIM:HEAD_DIM + 1])
    o_ref[0] = jnp.concatenate(outs, axis=0).T.astype(o_ref.dtype)


def _attend(qaug, kaug, vt4, nb):
    b, s, _ = qaug.shape
    return pl.pallas_call(
        _attend_kernel,
        out_shape=jax.ShapeDtypeStruct((b, s, WB), BF16),
        grid=(b, HEADS // ATT_HEADS, nb),
        in_specs=[pl.BlockSpec((1, BLOCK, ATT_HEADS * LANES), lambda bi, hg, i: (bi, i, hg)),
                  pl.BlockSpec((1, s, ATT_HEADS * LANES), lambda bi, hg, i: (bi, 0, hg)),
                  pl.BlockSpec((nb, 1, ATT_HEADS * V_ROWS, BLOCK), lambda bi, hg, i: (bi, 0, hg, 0))],
        out_specs=pl.BlockSpec((1, BLOCK, ATT_HEADS * HEAD_DIM), lambda bi, hg, i: (bi, i, hg)),
        scratch_shapes=[pltpu.VMEM((2, ATT_HEADS, BLOCK, BLOCK), F32),
                        pltpu.VMEM((ATT_HEADS, BLOCK, BLOCK), BF16),
                        pltpu.VMEM((ATT_HEADS, V_ROWS, BLOCK), F32),
                        pltpu.VMEM((ATT_HEADS, 1, BLOCK), F32),
                        pltpu.VMEM((ATT_HEADS, 1, BLOCK), F32),
                        pltpu.VMEM((2, ATT_HEADS, 1, BLOCK), F32)],
        compiler_params=_params(3),
        name="moba_attend",
    )(qaug, kaug, vt4)


def _sample_attend_kernel(n_pages, pt_ref, q_ref, kn_ref, vn_ref, *rest):
    del pt_ref
    kt_pages = rest[:n_pages]
    vt_pages = rest[n_pages:2 * n_pages]
    o_ref = rest[2 * n_pages]
    t_len = q_ref.shape[0]
    ppb = BLOCK // kt_pages[0].shape[3]
    nf = n_pages // ppb
    qbd = _block_diag_rows(q_ref[...], HEADS)
    lane = lax.broadcasted_iota(jnp.int32, (WB, nf), 1)
    kmt = jnp.zeros((WB, nf), F32)
    for jb in range(nf):
        blk = kt_pages[jb * ppb][0, 0]
        for p in range(jb * ppb + 1, (jb + 1) * ppb):
            blk = blk + kt_pages[p][0, 0]
        kmt = jnp.where(lane == jb, jnp.sum(blk, axis=1, keepdims=True) * (1.0 / BLOCK), kmt)
    q_hi, q_lo = _split_bf16(qbd)
    km_hi, km_lo = _split_bf16(kmt)
    g = _dot(q_hi, km_hi) + (_dot(q_lo, km_hi) + _dot(q_hi, km_lo))
    bias = _top_blocks_bias(g, jnp.ones(g.shape, jnp.bool_), 1)

    qs = (qbd * QK_SCALE).astype(BF16)
    s_pages = [_dot(qs, kp[0, 0].astype(BF16)) + bias[:, p // ppb:p // ppb + 1]
               for p, kp in enumerate(kt_pages)]
    r = lax.broadcasted_iota(jnp.int32, (HEADS * t_len, t_len), 0)
    c = lax.broadcasted_iota(jnp.int32, (HEADS * t_len, t_len), 1)
    s_own = jnp.where(c <= r % t_len, _dot_nt(qs, kn_ref[...].astype(BF16)), MASKED)
    m = jnp.max(s_own, axis=1, keepdims=True)
    for s in s_pages:
        m = jnp.maximum(m, jnp.max(s, axis=1, keepdims=True))
    p_own = jnp.exp2(s_own - m)
    l = jnp.sum(p_own, axis=1, keepdims=True)
    vn = vn_ref[...]
    acc = jnp.zeros((HEADS * t_len, WB), F32)
    for t in range(t_len):
        acc = acc + p_own[:, t:t + 1] * vn[t:t + 1, :]
    for s, vp in zip(s_pages, vt_pages):
        p = jnp.exp2(s - m)
        l = l + jnp.sum(p, axis=1, keepdims=True)
        acc = acc + _dot_nt(p.astype(BF16), vp[0, 0].astype(BF16))
    out = acc / l
    rr = lax.broadcasted_iota(jnp.int32, out.shape, 0)
    cc = lax.broadcasted_iota(jnp.int32, out.shape, 1)
    out = jnp.where((rr // t_len) == (cc // HEAD_DIM), out, 0.0)
    o_ref[...] = jnp.sum(out.reshape(HEADS, t_len, WB), axis=0)


def _sample_attend(layer, q2, kn2, vn2, cache_kt, cache_vt, page_table, t_len):
    db, n_pages = page_table.shape
    page = cache_kt.shape[3]
    assert BLOCK % page == 0 and n_pages % (BLOCK // page) == 0, "past length must be whole MoBA blocks"
    assert t_len <= BLOCK
    row = pl.BlockSpec((t_len, WB), lambda bi, pt: (bi, 0))

    def page_spec(p):
        return pl.BlockSpec((1, 1, WB, page), lambda bi, pt: (layer, pt[bi, p], 0, 0))

    pages = [page_spec(p) for p in range(n_pages)]
    return pl.pallas_call(
        functools.partial(_sample_attend_kernel, n_pages),
        out_shape=jax.ShapeDtypeStruct((db * t_len, WB), F32),
        grid_spec=pltpu.PrefetchScalarGridSpec(
            num_scalar_prefetch=1,
            grid=(db,),
            in_specs=[row, row, row] + pages + pages,
            out_specs=row,
        ),
        compiler_params=_params(1),
        name="moba_sample",
    )(page_table, q2, kn2, vn2, *([cache_kt] * n_pages), *([cache_vt] * n_pages))


def _merge_kernel(x_ref, u_ref, va_ref, za_ref, yb_ref, zb_ref, ga_ref, gb_ref,
                  ws_ref, bt_ref, wpa_ref, wpb_ref, wo_ref, o_ref, mix_ref):
    tm = x_ref.shape[0]
    gpl = LANES // (WA // A_GROUPS)
    r = lax.broadcasted_iota(jnp.int32, (CHUNK, CHUNK), 0)
    c = lax.broadcasted_iota(jnp.int32, (CHUNK, CHUNK), 1)
    lane = lax.broadcasted_iota(jnp.int32, (CHUNK, LANES), 1)
    for ch in range(tm // CHUNK):
        rows = slice(ch * CHUNK, (ch + 1) * CHUNK)
        for lg in range(WA // LANES):
            v = va_ref[rows, lg * LANES:(lg + 1) * LANES].astype(BF16)
            mixed = None
            for gi in range(gpl):
                w = ws_ref[lg * gpl + gi]
                w = jnp.where(r >= c, w, jnp.zeros_like(w))
                vg = jnp.where((lane // (LANES // gpl)) == gi, v, jnp.zeros_like(v))
                part = _dot(w, vg)
                mixed = part if mixed is None else mixed + part
            mix_ref[rows, lg * LANES:(lg + 1) * LANES] = mixed + bt_ref[:, lg * LANES:(lg + 1) * LANES]
    za = za_ref[...].astype(F32)
    zb = zb_ref[...].astype(F32)
    ya = u_ref[...].astype(F32) * mix_ref[...] * (za * jax.nn.sigmoid(za))
    yb = yb_ref[...].astype(F32) * (zb * jax.nn.sigmoid(zb))
    m = (jax.nn.sigmoid(ga_ref[...].astype(F32)) * _dot(ya.astype(BF16), wpa_ref[...])
         + jax.nn.sigmoid(gb_ref[...].astype(F32)) * _dot(yb.astype(BF16), wpb_ref[...]))
    o_ref[...] = x_ref[...] + _dot(m.astype(BF16), wo_ref[...])


def _merge(x2d, u, va, za, yb, zb, ga, gb, ws, bt, wpa, wpb, wo):
    rows = x2d.shape[0]
    tm = 256
    row = lambda w_: pl.BlockSpec((tm, w_), lambda i: (i, 0))
    full = lambda a: pl.BlockSpec(a.shape, lambda i: (0,) * a.ndim)
    return pl.pallas_call(
        _merge_kernel,
        out_shape=jax.ShapeDtypeStruct((rows, D_MODEL), F32),
        grid=(rows // tm,),
        in_specs=[row(D_MODEL), row(WA), row(WA), row(WA), row(WB), row(WB), row(D_MODEL), row(D_MODEL),
                  full(ws), full(bt), full(wpa), full(wpb), full(wo)],
        out_specs=row(D_MODEL),
        scratch_shapes=[pltpu.VMEM((tm, WA), F32)],
        compiler_params=_params(1),
        name="merge",
    )(x2d, u, va, za, yb, zb, ga, gb, ws, bt, wpa, wpb, wo)


def _rope_tables(pos):
    half = HEAD_DIM // 2
    inv = jnp.power(jnp.float32(ROPE_THETA), -jnp.arange(half, dtype=F32) * (2.0 / HEAD_DIM))
    ang = pos.astype(F32)[:, None] * inv[None, :]
    cos, sin = jnp.cos(ang), jnp.sin(ang)
    reps = LANES // HEAD_DIM
    return (jnp.tile(jnp.concatenate([cos, cos], axis=1), (1, reps)),
            jnp.tile(jnp.concatenate([-sin, sin], axis=1), (1, reps)))


def _spread_matrix():
    m = np.zeros((WB, AUG), np.float32)
    c = np.arange(WB)
    m[c, (c // HEAD_DIM) * LANES + c % HEAD_DIM] = 1.0
    return jnp.asarray(m, BF16)


def _place_matrix(nb):
    m = np.zeros((HEADS * nb, AUG), np.float32)
    r = np.arange(HEADS * nb)
    m[r, (r // nb) * LANES + HEAD_DIM + r % nb] = 1.0
    return jnp.asarray(m, BF16)


def kernel(x_prompt, x_sample, cache_k, cache_v, page_table, norm_g, w_in, v_norm_g, w_s, b_s,
           q_norm_g, k_norm_g, w_pa, w_pb, w_o):
    depth = w_in.shape[0]
    b, s, _ = x_prompt.shape
    db, t_len, _ = x_sample.shape
    n_pool, page = cache_k.shape[1], cache_k.shape[2]
    past_len = page_table.shape[1] * page
    assert s % BLOCK == 0 and (db * t_len) % BLOCK == 0 and CHUNK % t_len == 0
    nb = s // BLOCK
    assert nb <= LANES - HEAD_DIM, "one spare lane per key block"

    cos_p, sin_p = _rope_tables(jnp.arange(s))
    cos_s, sin_s = _rope_tables(past_len + jnp.arange(db * t_len) % t_len)
    hsum = (jnp.arange(WB)[:, None] // HEAD_DIM == jnp.arange(WB)[None, :] // HEAD_DIM).astype(BF16)
    spread = _spread_matrix()
    place = _place_matrix(nb)
    cache_kt = cache_k.transpose(0, 1, 3, 4, 2).reshape(depth, n_pool, WB, page)
    cache_vt = cache_v.transpose(0, 1, 3, 4, 2).reshape(depth, n_pool, WB, page)
    cw = WA // A_GROUPS

    xp = x_prompt.reshape(b * s, D_MODEL)
    xs = x_sample.reshape(db * t_len, D_MODEL)
    cp_l, ks_l, vs_l, cs_l = [], [], [], []
    kv_new = (jnp.zeros((depth, b, WB, s), F32), jnp.zeros((depth, b, WB, s), F32))
    for l in range(depth):
        ng = norm_g[l][None, :]
        vg = v_norm_g[l][None, :]
        qg = jnp.tile(q_norm_g[l], HEADS)[None, :]
        kg = jnp.tile(k_norm_g[l], HEADS)[None, :]
        w = w_in[l].astype(BF16)
        wpa, wpb, wo = w_pa[l].astype(BF16), w_pb[l].astype(BF16), w_o[l].astype(BF16)
        ws_p = w_s[l].astype(BF16)
        bt_p = jnp.repeat(b_s[l].T, cw, axis=1)
        eye = jnp.eye(CHUNK // t_len, dtype=F32)
        ws_s = jnp.stack([jnp.kron(eye, w_s[l, g, :t_len, :t_len]) for g in range(A_GROUPS)]).astype(BF16)
        bt_s = jnp.tile(bt_p[:t_len], (CHUNK // t_len, 1))

        u, va, za, q, zb, ga, gb, kaug, vt, km, kt_new, vt_new = _proj(
            True, xp, s, ng, w, vg, qg, kg, cos_p, sin_p, hsum, spread, l, depth, kv_new)
        kv_new = (kt_new, vt_new)
        qaug = _gate(q.reshape(b, s, WB), km.reshape(b, nb, WB), spread, place)
        yb = _attend(qaug, kaug.reshape(b, s, AUG), vt, nb).reshape(b * s, WB)
        xp = _merge(xp, u, va, za, yb, zb, ga, gb, ws_p, bt_p, wpa, wpb, wo)
        cp_l.append(va.reshape(b, s, WA)[:, s - CHUNK:])

        u, va, za, q, zb, ga, gb, k, vb = _proj(False, xs, t_len, ng, w, vg, qg, kg, cos_s, sin_s, hsum)
        yb = _sample_attend(l, q, k, vb, cache_kt, cache_vt, page_table, t_len)
        xs = _merge(xs, u, va, za, yb, zb, ga, gb, ws_s, bt_s, wpa, wpb, wo)
        ks_l.append(k.reshape(db, t_len, HEADS, HEAD_DIM))
        vs_l.append(vb.reshape(db, t_len, HEADS, HEAD_DIM))
        cs_l.append(va.reshape(db, t_len, WA))

    def untranspose(t):
        return t.reshape(depth, b, HEADS, HEAD_DIM, s).transpose(0, 1, 4, 2, 3)

    return (xp.reshape(b, s, D_MODEL), xs.reshape(db, t_len, D_MODEL),
            untranspose(kv_new[0]), untranspose(kv_new[1]), jnp.stack(cp_l),
            jnp.stack(ks_l), jnp.stack(vs_l), jnp.stack(cs_l))
```

```python
import functools
import math

import jax
import jax.numpy as jnp
import numpy as np
from jax import lax
from jax.experimental import pallas as pl
from jax.experimental.pallas import tpu as pltpu

D_MODEL = 1024
WA = 512
A_GROUPS = 8
CHUNK = 128
HEADS = 8
HEAD_DIM = 64
WB = HEADS * HEAD_DIM
BLOCK = 256
TOPK = 3
ROPE_THETA = 10000.0
EPS = 1e-6
OFF = np.cumsum((0, WA, WA, WA, WB, WB, WB, WB, D_MODEL)).tolist()

LANES = 128
AUG = HEADS * LANES
VMEM_LIMIT = 56 * 1024 * 1024
ATT_HEADS = 4
SAMPLES_PER_STEP = 2
BF16_SUBLANES = 16
V_ROWS = HEAD_DIM + BF16_SUBLANES

F32 = jnp.float32
BF16 = jnp.bfloat16
NEG_INF = float("-inf")
MASKED = -1e30
QK_SCALE = HEAD_DIM ** -0.5 * math.log2(math.e)


def _dot(a, b):
    return jnp.dot(a, b, preferred_element_type=F32)


def _dot_nt(a, b):
    return lax.dot_general(a, b, (((1,), (1,)), ((), ())), preferred_element_type=F32)


def _split_bf16(x):
    hi = x.astype(BF16)
    lo = (x - hi.astype(F32)).astype(BF16)
    return hi, lo


def _params(n_axes):
    return pltpu.CompilerParams(dimension_semantics=("arbitrary",) * n_axes, vmem_limit_bytes=VMEM_LIMIT)


def _head_norm_rope(p, gain, hsum, cos, sin, first_half):
    ms = _dot((p * p).astype(BF16), hsum) * (1.0 / HEAD_DIM)
    xn = p * lax.rsqrt(ms + EPS) * gain
    partner = jnp.where(first_half, pltpu.roll(xn, WB - HEAD_DIM // 2, 1), pltpu.roll(xn, HEAD_DIM // 2, 1))
    return xn * cos + partner * sin


def _proj_kernel(prompt, nb, x_ref, ng_ref, w_ref, vg_ref, qg_ref, kg_ref, cos_ref, sin_ref, hsum_ref, *rest):
    if prompt:
        spread_ref = rest[0]
        (u_ref, va_ref, za_ref, q_ref, zb_ref, ga_ref, gb_ref,
         kaug_ref, vt_ref, km_ref, kt_ref, vtf_ref) = rest[-12:]
    else:
        u_ref, va_ref, za_ref, q_ref, zb_ref, ga_ref, gb_ref, k_ref, vb_ref = rest
    x = x_ref[...]
    ms = jnp.mean(x * x, axis=-1, keepdims=True)
    h = (x * lax.rsqrt(ms + EPS) * ng_ref[...]).astype(BF16)

    def col(i, width):
        return _dot(h, w_ref[:, OFF[i]:OFF[i] + width])

    u_ref[...] = col(0, WA).astype(u_ref.dtype)
    va = col(1, WA)
    va_ref[...] = va * lax.rsqrt(jnp.mean(va * va, axis=-1, keepdims=True) + EPS) * vg_ref[...]
    za_ref[...] = col(2, WA).astype(za_ref.dtype)

    reps = WB // LANES
    cos = jnp.concatenate([cos_ref[...]] * reps, axis=1)
    sin = jnp.concatenate([sin_ref[...]] * reps, axis=1)
    lane = lax.broadcasted_iota(jnp.int32, (1, WB), 1)
    first_half = (lane % HEAD_DIM) < (HEAD_DIM // 2)
    hsum = hsum_ref[...]
    q_ref[...] = _head_norm_rope(col(3, WB), qg_ref[...], hsum, cos, sin, first_half)
    k = _head_norm_rope(col(4, WB), kg_ref[...], hsum, cos, sin, first_half)
    vb = col(5, WB)
    zb_ref[...] = col(6, WB).astype(zb_ref.dtype)
    ga_ref[...] = col(7, D_MODEL).astype(ga_ref.dtype)
    gb_ref[...] = col(8, D_MODEL).astype(gb_ref.dtype)
    if not prompt:
        k_ref[...] = k
        vb_ref[...] = vb
    else:
        j = pl.program_id(0) % nb
        lane_a = lax.broadcasted_iota(jnp.int32, (1, AUG), 1)
        mark = jnp.where(lane_a % LANES == HEAD_DIM + j, 1.0, 0.0)
        kaug_ref[...] = (_dot(k.astype(BF16), spread_ref[...]) + mark).astype(BF16)
        kt_ref[0, 0] = k.T
        vtf = vb.T
        vtf_ref[0, 0] = vtf
        vt = vtf.astype(BF16)
        tail = lax.broadcasted_iota(jnp.int32, (BF16_SUBLANES, vt.shape[1]), 0)
        ones_row = jnp.where(tail == 0, 1.0, 0.0).astype(BF16)
        parts = []
        for hd in range(HEADS):
            parts += [vt[hd * HEAD_DIM:(hd + 1) * HEAD_DIM], ones_row]
        vt_ref[0, 0] = jnp.concatenate(parts, axis=0)
        km_ref[0] = jnp.mean(k, axis=0, keepdims=True)


def _proj(prompt, x2d, seq, ng, w, vg, qg, kg, cos, sin, hsum, spread=None, layer=0, depth=1, kv_prev=None):
    rows = x2d.shape[0]
    tm = BLOCK
    nt = rows // tm
    tiles_per_seq = seq // tm if prompt else nt
    row = lambda w_: pl.BlockSpec((tm, w_), lambda i: (i, 0))
    full = lambda a: pl.BlockSpec(a.shape, lambda i: (0,) * a.ndim)
    tab = pl.BlockSpec((tm, LANES), lambda i: (i % tiles_per_seq, 0))
    widths = (WA, WA, WA, WB, WB, D_MODEL, D_MODEL)
    dtypes = (BF16, F32, BF16, F32, BF16, BF16, BF16)
    out_shape = [jax.ShapeDtypeStruct((rows, w_), d_) for w_, d_ in zip(widths, dtypes)]
    out_specs = [row(w_) for w_ in widths]
    args = [x2d, ng, w, vg, qg, kg, cos, sin, hsum]
    in_specs = [row(D_MODEL), full(ng), full(w), full(vg), full(qg), full(kg), tab, tab, full(hsum)]
    aliases = {}
    if prompt:
        args.append(spread)
        in_specs.append(full(spread))
        batch = rows // seq
        kv_shape = jax.ShapeDtypeStruct((depth, batch, WB, seq), F32)
        kv_spec = pl.BlockSpec((1, 1, WB, tm), lambda i: (layer, i // tiles_per_seq, 0, i % tiles_per_seq))
        out_shape += [jax.ShapeDtypeStruct((rows, AUG), BF16),
                      jax.ShapeDtypeStruct((nt, 1, HEADS * V_ROWS, tm), BF16),
                      jax.ShapeDtypeStruct((nt, 1, WB), F32),
                      kv_shape, kv_shape]
        out_specs += [row(AUG),
                      pl.BlockSpec((1, 1, HEADS * V_ROWS, tm), lambda i: (i, 0, 0, 0)),
                      pl.BlockSpec((1, 1, WB), lambda i: (i, 0, 0)),
                      kv_spec, kv_spec]
        aliases = {len(args): len(out_shape) - 2, len(args) + 1: len(out_shape) - 1}
        args += list(kv_prev)
        in_specs += [pl.BlockSpec(memory_space=pl.ANY)] * 2
    else:
        out_shape += [jax.ShapeDtypeStruct((rows, WB), F32)] * 2
        out_specs += [row(WB)] * 2
    return pl.pallas_call(
        functools.partial(_proj_kernel, prompt, tiles_per_seq),
        out_shape=out_shape,
        grid=(nt,),
        in_specs=in_specs,
        out_specs=out_specs,
        input_output_aliases=aliases,
        compiler_params=_params(1),
        name="proj_prompt" if prompt else "proj_sample",
    )(*args)


def _top_blocks_bias(g, valid, axis):
    idx = lax.broadcasted_iota(jnp.int32, g.shape, axis).astype(F32)
    n = float(g.shape[axis])
    g = jnp.where(valid, g, NEG_INF)
    sel = jnp.zeros(g.shape, jnp.bool_)
    for _ in range(min(TOPK, g.shape[axis])):
        m = jnp.max(g, axis=axis, keepdims=True)
        first = jnp.min(jnp.where(g == m, idx, n), axis=axis, keepdims=True)
        pick = (idx == first) & (m > NEG_INF)
        sel = sel | pick
        g = jnp.where(pick, NEG_INF, g)
    return jnp.where(sel, 0.0, MASKED).astype(F32)


def _block_diag_rows(km, heads):
    n = km.shape[0]
    t = jnp.concatenate([km] * heads, axis=0)
    r = lax.broadcasted_iota(jnp.int32, t.shape, 0)
    c = lax.broadcasted_iota(jnp.int32, t.shape, 1)
    return jnp.where((r // n) == (c // HEAD_DIM), t, jnp.zeros_like(t))


def _gate_kernel(tq, q_ref, km_ref, spread_t_ref, place_t_ref, qaug_ref):
    t = pl.program_id(1)
    nb = km_ref.shape[1]
    q = q_ref[0]
    km_hi, km_lo = _split_bf16(_block_diag_rows(km_ref[0], HEADS))
    q_hi, q_lo = _split_bf16(q)
    g = _dot_nt(km_hi, q_hi) + (_dot_nt(km_hi, q_lo) + _dot_nt(km_lo, q_hi))
    g = g.reshape(HEADS, nb, tq)
    j = lax.broadcasted_iota(jnp.int32, g.shape, 1)
    pos = t * tq + lax.broadcasted_iota(jnp.int32, g.shape, 2)
    bias = _top_blocks_bias(g, j < pos // BLOCK, 1)
    bias = jnp.where(j == pos // BLOCK, 0.0, bias).reshape(HEADS * nb, tq)
    qs = (q * QK_SCALE).astype(BF16)
    qaug_t = _dot_nt(spread_t_ref[...], qs) + _dot(place_t_ref[...], bias.astype(BF16))
    qaug_ref[0] = qaug_t.astype(BF16)


def _gate(q3, km3, spread_t, place_t):
    b, s, _ = q3.shape
    nb = km3.shape[1]
    tq = min(512, s)
    assert s % tq == 0
    full = lambda a: pl.BlockSpec(a.shape, lambda bi, t: (0,) * a.ndim)
    return pl.pallas_call(
        functools.partial(_gate_kernel, tq),
        out_shape=jax.ShapeDtypeStruct((b, AUG, s), BF16),
        grid=(b, s // tq),
        in_specs=[pl.BlockSpec((1, tq, WB), lambda bi, t: (bi, t, 0)),
                  pl.BlockSpec((1, nb, WB), lambda bi, t: (bi, 0, 0)),
                  full(spread_t), full(place_t)],
        out_specs=pl.BlockSpec((1, AUG, tq), lambda bi, t: (bi, 0, t)),
        compiler_params=_params(2),
        name="moba_gate",
    )(q3, km3, spread_t, place_t)


def _attend_kernel(q_ref, k_ref, vt_ref, o_ref, s_scr, p_scr, acc_ref, m_ref, alpha_ref, cmax_ref):
    i = pl.program_id(2)
    heads = range(ATT_HEADS)

    def scores(blk, h):
        start = pl.multiple_of(blk * BLOCK, BLOCK)
        grp = slice(h * LANES, (h + 1) * LANES)
        return _dot(k_ref[0, pl.ds(start, BLOCK), grp], q_ref[0, grp, :])

    def pv(blk, h):
        return _dot(vt_ref[blk, 0, h * V_ROWS:(h + 1) * V_ROWS, :], p_scr[h])

    kk = lax.broadcasted_iota(jnp.int32, (BLOCK, BLOCK), 0)
    qq = lax.broadcasted_iota(jnp.int32, (BLOCK, BLOCK), 1)
    causal = jnp.where(kk <= qq, 0.0, MASKED).astype(F32)
    for h in heads:
        s = scores(i, h) + causal
        s_scr[0, h] = s
        cmax_ref[0, h] = jnp.max(s, axis=0, keepdims=True)
        p_scr[h] = jnp.zeros((BLOCK, BLOCK), BF16)
        acc_ref[h] = jnp.zeros((V_ROWS, BLOCK), F32)
        m_ref[h] = jnp.full((1, BLOCK), MASKED, F32)
        alpha_ref[h] = jnp.zeros((1, BLOCK), F32)

    def block_of(t):
        return jnp.where(t <= 0, i, t - 1)

    def step(t, cur, nxt):
        pvs = [pv(block_of(t - 1), h) for h in heads]
        for h in heads:
            s = scores(t, h)
            s_scr[nxt, h] = s
            cmax_ref[nxt, h] = jnp.max(s, axis=0, keepdims=True)
        for h in heads:
            m_old = m_ref[h]
            m_new = jnp.maximum(m_old, cmax_ref[cur, h])
            acc_ref[h] = alpha_ref[h] * acc_ref[h] + pvs[h]
            alpha_ref[h] = jnp.exp2(m_old - m_new)
            m_ref[h] = m_new
            p_scr[h] = jnp.exp2(s_scr[cur, h] - m_new).astype(BF16)

    def run(first, count, unroll):
        def body(u, carry):
            for k in range(unroll):
                step(first + unroll * u + k, k % 2, (k + 1) % 2)
            return carry
        lax.fori_loop(0, count, body, 0)

    n_steps = i + 1
    quads = n_steps // 4
    run(0, quads, 4)
    run(4 * quads, (n_steps % 4) // 2, 2)
    run(n_steps - n_steps % 2, n_steps % 2, 1)

    outs = []
    for h in heads:
        acc = alpha_ref[h] * acc_ref[h] + pv(block_of(i), h)
        outs.append(acc[:HEAD_DIM] / acc[HEAD_DIM:HEAD_DIM + 1])
    o_ref[0] = jnp.concatenate(outs, axis=0).T.astype(o_ref.dtype)


def _attend(qaug_t, kaug, vt4, nb):
    b, s, _ = kaug.shape
    return pl.pallas_call(
        _attend_kernel,
        out_shape=jax.ShapeDtypeStruct((b, s, WB), BF16),
        grid=(b, HEADS // ATT_HEADS, nb),
        in_specs=[pl.BlockSpec((1, ATT_HEADS * LANES, BLOCK), lambda bi, hg, i: (bi, hg, i)),
                  pl.BlockSpec((1, s, ATT_HEADS * LANES), lambda bi, hg, i: (bi, 0, hg)),
                  pl.BlockSpec((nb, 1, ATT_HEADS * V_ROWS, BLOCK), lambda bi, hg, i: (bi, 0, hg, 0))],
        out_specs=pl.BlockSpec((1, BLOCK, ATT_HEADS * HEAD_DIM), lambda bi, hg, i: (bi, i, hg)),
        scratch_shapes=[pltpu.VMEM((2, ATT_HEADS, BLOCK, BLOCK), F32),
                        pltpu.VMEM((ATT_HEADS, BLOCK, BLOCK), BF16),
                        pltpu.VMEM((ATT_HEADS, V_ROWS, BLOCK), F32),
                        pltpu.VMEM((ATT_HEADS, 1, BLOCK), F32),
                        pltpu.VMEM((ATT_HEADS, 1, BLOCK), F32),
                        pltpu.VMEM((2, ATT_HEADS, 1, BLOCK), F32)],
        compiler_params=_params(3),
        name="moba_attend",
    )(qaug_t, kaug, vt4)


def _sample_attend_one(q, kn, vn, kt_pages, vt_pages):
    n_pages = len(kt_pages)
    t_len = q.shape[0]
    ppb = BLOCK // kt_pages[0].shape[3]
    nf = n_pages // ppb
    qbd = _block_diag_rows(q, HEADS)
    lane = lax.broadcasted_iota(jnp.int32, (WB, nf), 1)
    kmt = jnp.zeros((WB, nf), F32)
    for jb in range(nf):
        blk = kt_pages[jb * ppb][0, 0]
        for p in range(jb * ppb + 1, (jb + 1) * ppb):
            blk = blk + kt_pages[p][0, 0]
        kmt = jnp.where(lane == jb, jnp.sum(blk, axis=1, keepdims=True) * (1.0 / BLOCK), kmt)
    yield None
    q_hi, q_lo = _split_bf16(qbd)
    km_hi, km_lo = _split_bf16(kmt)
    g = _dot(q_hi, km_hi) + (_dot(q_lo, km_hi) + _dot(q_hi, km_lo))
    bias = _top_blocks_bias(g, jnp.ones(g.shape, jnp.bool_), 1)

    yield None
    qs = (qbd * QK_SCALE).astype(BF16)
    page = kt_pages[0].shape[3]
    bias_b = [jnp.broadcast_to(bias[:, jb:jb + 1], (HEADS * t_len, page)) for jb in range(nf)]
    s_pages = [_dot(qs, kp[0, 0].astype(BF16)) + bias_b[p // ppb]
               for p, kp in enumerate(kt_pages)]
    r = lax.broadcasted_iota(jnp.int32, (HEADS * t_len, t_len), 0)
    c = lax.broadcasted_iota(jnp.int32, (HEADS * t_len, t_len), 1)
    s_own = jnp.where(c <= r % t_len, _dot_nt(qs, kn.astype(BF16)), MASKED)
    m_fold = s_pages[0]
    for s in s_pages[1:]:
        m_fold = jnp.maximum(m_fold, s)
    m = jnp.maximum(jnp.max(s_own, axis=1, keepdims=True), jnp.max(m_fold, axis=1, keepdims=True))
    yield None
    p_own = jnp.exp2(s_own - m)
    acc = jnp.zeros((HEADS * t_len, WB), F32)
    for t in range(t_len):
        acc = acc + p_own[:, t:t + 1] * vn[t:t + 1, :]
    l_fold = jnp.zeros((HEADS * t_len, page), F32)
    for s, vp in zip(s_pages, vt_pages):
        p = jnp.exp2(s - m)
        l_fold = l_fold + p
        acc = acc + _dot_nt(p.astype(BF16), vp[0, 0].astype(BF16))
    l = jnp.sum(p_own, axis=1, keepdims=True) + jnp.sum(l_fold, axis=1, keepdims=True)
    out = acc / l
    rr = lax.broadcasted_iota(jnp.int32, out.shape, 0)
    cc = lax.broadcasted_iota(jnp.int32, out.shape, 1)
    out = jnp.where((rr // t_len) == (cc // HEAD_DIM), out, 0.0)
    yield jnp.sum(out.reshape(HEADS, t_len, WB), axis=0)


SAMPLE_PHASES = 4


def _sample_attend_kernel(n_pages, t_len, pt_ref, q_ref, kn_ref, vn_ref, *rest):
    del pt_ref
    o_ref = rest[-1]
    rows = [slice(i * t_len, (i + 1) * t_len) for i in range(SAMPLES_PER_STEP)]
    runs = [_sample_attend_one(q_ref[r, :], kn_ref[r, :], vn_ref[r, :],
                               rest[2 * i * n_pages:(2 * i + 1) * n_pages],
                               rest[(2 * i + 1) * n_pages:(2 * i + 2) * n_pages])
            for i, r in enumerate(rows)]
    for _ in range(SAMPLE_PHASES - 1):
        for run in runs:
            next(run)
    for r, run in zip(rows, runs):
        o_ref[r, :] = next(run)


def _sample_attend(layer, q2, kn2, vn2, cache_kt, cache_vt, page_table, t_len):
    db, n_pages = page_table.shape
    page = cache_kt.shape[3]
    assert BLOCK % page == 0 and n_pages % (BLOCK // page) == 0, "past length must be whole MoBA blocks"
    assert t_len <= BLOCK and db % SAMPLES_PER_STEP == 0
    row = pl.BlockSpec((SAMPLES_PER_STEP * t_len, WB), lambda bi, pt: (bi, 0))

    def page_spec(i, p):
        return pl.BlockSpec((1, 1, WB, page), lambda bi, pt: (layer, pt[bi * SAMPLES_PER_STEP + i, p], 0, 0))

    pages, page_args = [], []
    for i in range(SAMPLES_PER_STEP):
        pages += [page_spec(i, p) for p in range(n_pages)] * 2
        page_args += [cache_kt] * n_pages + [cache_vt] * n_pages
    return pl.pallas_call(
        functools.partial(_sample_attend_kernel, n_pages, t_len),
        out_shape=jax.ShapeDtypeStruct((db * t_len, WB), F32),
        grid_spec=pltpu.PrefetchScalarGridSpec(
            num_scalar_prefetch=1,
            grid=(db // SAMPLES_PER_STEP,),
            in_specs=[row, row, row] + pages,
            out_specs=row,
        ),
        compiler_params=_params(1),
        name="moba_sample",
    )(page_table, q2, kn2, vn2, *page_args)


def _merge_kernel(x_ref, u_ref, va_ref, za_ref, yb_ref, zb_ref, ga_ref, gb_ref,
                  ws_ref, bt_ref, wpa_ref, wpb_ref, wo_ref, o_ref, mix_ref):
    tm = x_ref.shape[0]
    gpl = LANES // (WA // A_GROUPS)
    r = lax.broadcasted_iota(jnp.int32, (CHUNK, CHUNK), 0)
    c = lax.broadcasted_iota(jnp.int32, (CHUNK, CHUNK), 1)
    lane = lax.broadcasted_iota(jnp.int32, (CHUNK, LANES), 1)
    for ch in range(tm // CHUNK):
        rows = slice(ch * CHUNK, (ch + 1) * CHUNK)
        for lg in range(WA // LANES):
            v = va_ref[rows, lg * LANES:(lg + 1) * LANES].astype(BF16)
            mixed = None
            for gi in range(gpl):
                w = ws_ref[lg * gpl + gi]
                w = jnp.where(r >= c, w, jnp.zeros_like(w))
                vg = jnp.where((lane // (LANES // gpl)) == gi, v, jnp.zeros_like(v))
                part = _dot(w, vg)
                mixed = part if mixed is None else mixed + part
            mix_ref[rows, lg * LANES:(lg + 1) * LANES] = mixed + bt_ref[:, lg * LANES:(lg + 1) * LANES]
    za = za_ref[...].astype(F32)
    zb = zb_ref[...].astype(F32)
    ya = u_ref[...].astype(F32) * mix_ref[...] * (za * jax.nn.sigmoid(za))
    yb = yb_ref[...].astype(F32) * (zb * jax.nn.sigmoid(zb))
    m = (jax.nn.sigmoid(ga_ref[...].astype(F32)) * _dot(ya.astype(BF16), wpa_ref[...])
         + jax.nn.sigmoid(gb_ref[...].astype(F32)) * _dot(yb.astype(BF16), wpb_ref[...]))
    o_ref[...] = x_ref[...] + _dot(m.astype(BF16), wo_ref[...])


def _merge(x2d, u, va, za, yb, zb, ga, gb, ws, bt, wpa, wpb, wo):
    rows = x2d.shape[0]
    tm = 256
    row = lambda w_: pl.BlockSpec((tm, w_), lambda i: (i, 0))
    full = lambda a: pl.BlockSpec(a.shape, lambda i: (0,) * a.ndim)
    return pl.pallas_call(
        _merge_kernel,
        out_shape=jax.ShapeDtypeStruct((rows, D_MODEL), F32),
        grid=(rows // tm,),
        in_specs=[row(D_MODEL), row(WA), row(WA), row(WA), row(WB), row(WB), row(D_MODEL), row(D_MODEL),
                  full(ws), full(bt), full(wpa), full(wpb), full(wo)],
        out_specs=row(D_MODEL),
        scratch_shapes=[pltpu.VMEM((tm, WA), F32)],
        compiler_params=_params(1),
        name="merge",
    )(x2d, u, va, za, yb, zb, ga, gb, ws, bt, wpa, wpb, wo)


def _rope_tables(pos):
    half = HEAD_DIM // 2
    inv = jnp.power(jnp.float32(ROPE_THETA), -jnp.arange(half, dtype=F32) * (2.0 / HEAD_DIM))
    ang = pos.astype(F32)[:, None] * inv[None, :]
    cos, sin = jnp.cos(ang), jnp.sin(ang)
    reps = LANES // HEAD_DIM
    return (jnp.tile(jnp.concatenate([cos, cos], axis=1), (1, reps)),
            jnp.tile(jnp.concatenate([-sin, sin], axis=1), (1, reps)))


def _spread_matrix():
    m = np.zeros((WB, AUG), np.float32)
    c = np.arange(WB)
    m[c, (c // HEAD_DIM) * LANES + c % HEAD_DIM] = 1.0
    return jnp.asarray(m, BF16)


def _place_matrix_t(nb):
    m = np.zeros((AUG, HEADS * nb), np.float32)
    r = np.arange(HEADS * nb)
    m[(r // nb) * LANES + HEAD_DIM + r % nb, r] = 1.0
    return jnp.asarray(m, BF16)


def kernel(x_prompt, x_sample, cache_k, cache_v, page_table, norm_g, w_in, v_norm_g, w_s, b_s,
           q_norm_g, k_norm_g, w_pa, w_pb, w_o):
    depth = w_in.shape[0]
    b, s, _ = x_prompt.shape
    db, t_len, _ = x_sample.shape
    n_pool, page = cache_k.shape[1], cache_k.shape[2]
    past_len = page_table.shape[1] * page
    assert s % BLOCK == 0 and (db * t_len) % BLOCK == 0 and CHUNK % t_len == 0
    nb = s // BLOCK
    assert nb <= LANES - HEAD_DIM, "one spare lane per key block"

    cos_p, sin_p = _rope_tables(jnp.arange(s))
    cos_s, sin_s = _rope_tables(past_len + jnp.arange(db * t_len) % t_len)
    hsum = (jnp.arange(WB)[:, None] // HEAD_DIM == jnp.arange(WB)[None, :] // HEAD_DIM).astype(BF16)
    spread = _spread_matrix()
    spread_t = spread.T
    place_t = _place_matrix_t(nb)
    cache_kt = cache_k.transpose(0, 1, 3, 4, 2).reshape(depth, n_pool, WB, page)
    cache_vt = cache_v.transpose(0, 1, 3, 4, 2).reshape(depth, n_pool, WB, page)
    cw = WA // A_GROUPS

    xp = x_prompt.reshape(b * s, D_MODEL)
    xs = x_sample.reshape(db * t_len, D_MODEL)
    cp_l, ks_l, vs_l, cs_l = [], [], [], []
    kv_new = (jnp.zeros((depth, b, WB, s), F32), jnp.zeros((depth, b, WB, s), F32))
    for l in range(depth):
        ng = norm_g[l][None, :]
        vg = v_norm_g[l][None, :]
        qg = jnp.tile(q_norm_g[l], HEADS)[None, :]
        kg = jnp.tile(k_norm_g[l], HEADS)[None, :]
        w = w_in[l].astype(BF16)
        wpa, wpb, wo = w_pa[l].astype(BF16), w_pb[l].astype(BF16), w_o[l].astype(BF16)
        ws_p = w_s[l].astype(BF16)
        bt_p = jnp.repeat(b_s[l].T, cw, axis=1)
        eye = jnp.eye(CHUNK // t_len, dtype=F32)
        ws_s = jnp.stack([jnp.kron(eye, w_s[l, g, :t_len, :t_len]) for g in range(A_GROUPS)]).astype(BF16)
        bt_s = jnp.tile(bt_p[:t_len], (CHUNK // t_len, 1))

        u, va, za, q, zb, ga, gb, kaug, vt, km, kt_new, vt_new = _proj(
            True, xp, s, ng, w, vg, qg, kg, cos_p, sin_p, hsum, spread, l, depth, kv_new)
        kv_new = (kt_new, vt_new)
        qaug = _gate(q.reshape(b, s, WB), km.reshape(b, nb, WB), spread_t, place_t)
        yb = _attend(qaug, kaug.reshape(b, s, AUG), vt, nb).reshape(b * s, WB)
        xp = _merge(xp, u, va, za, yb, zb, ga, gb, ws_p, bt_p, wpa, wpb, wo)
        cp_l.append(va.reshape(b, s, WA)[:, s - CHUNK:])

        u, va, za, q, zb, ga, gb, k, vb = _proj(False, xs, t_len, ng, w, vg, qg, kg, cos_s, sin_s, hsum)
        yb = _sample_attend(l, q, k, vb, cache_kt, cache_vt, page_table, t_len)
        xs = _merge(xs, u, va, za, yb, zb, ga, gb, ws_s, bt_s, wpa, wpb, wo)
        ks_l.append(k.reshape(db, t_len, HEADS, HEAD_DIM))
        vs_l.append(vb.reshape(db, t_len, HEADS, HEAD_DIM))
        cs_l.append(va.reshape(db, t_len, WA))

    def untranspose(t):
        return t.reshape(depth, b, HEADS, HEAD_DIM, s).transpose(0, 1, 4, 2, 3)

    return (xp.reshape(b, s, D_MODEL), xs.reshape(db, t_len, D_MODEL),
            untranspose(kv_new[0]), untranspose(kv_new[1]), jnp.stack(cp_l),
            jnp.stack(ks_l), jnp.stack(vs_l), jnp.stack(cs_l))
```

```python
import functools
import math

import jax
import jax.numpy as jnp
import numpy as np
from jax import lax
from jax.experimental import pallas as pl
from jax.experimental.pallas import tpu as pltpu

D_MODEL = 1024
WA = 512
A_GROUPS = 8
CHUNK = 128
HEADS = 8
HEAD_DIM = 64
WB = HEADS * HEAD_DIM
BLOCK = 256
TOPK = 3
ROPE_THETA = 10000.0
EPS = 1e-6
OFF = np.cumsum((0, WA, WA, WA, WB, WB, WB, WB, D_MODEL)).tolist()

LANES = 128
MXU_DEPTH = 256
AUG = HEADS * LANES
VMEM_LIMIT = 56 * 1024 * 1024
ATT_HEADS = 4
SAMPLES_PER_STEP = 2
BF16_SUBLANES = 16
V_ROWS = HEAD_DIM + BF16_SUBLANES

F32 = jnp.float32
BF16 = jnp.bfloat16
NEG_INF = float("-inf")
MASKED = -1e30
QK_SCALE = HEAD_DIM ** -0.5 * math.log2(math.e)


def _dot(a, b):
    return jnp.dot(a, b, preferred_element_type=F32)


def _dot_nt(a, b):
    return lax.dot_general(a, b, (((1,), (1,)), ((), ())), preferred_element_type=F32)


def _dot_block_diag(a, m):
    parts = m.shape[0] // MXU_DEPTH
    nc = m.shape[1] // parts
    return jnp.concatenate(
        [_dot(a[:, i * MXU_DEPTH:(i + 1) * MXU_DEPTH], m[i * MXU_DEPTH:(i + 1) * MXU_DEPTH, i * nc:(i + 1) * nc])
         for i in range(parts)], axis=1)


def _split_bf16(x):
    hi = x.astype(BF16)
    lo = (x - hi.astype(F32)).astype(BF16)
    return hi, lo


def _params(n_axes):
    return pltpu.CompilerParams(dimension_semantics=("arbitrary",) * n_axes, vmem_limit_bytes=VMEM_LIMIT)


def _layer_spec(layer, a):
    return pl.BlockSpec((None,) + a.shape[1:], lambda i: (layer,) + (0,) * (a.ndim - 1))


def _head_norm_rope(p, gain, hsum, cos, sin, first_half):
    ms = _dot_block_diag((p * p).astype(BF16), hsum) * (1.0 / HEAD_DIM)
    xn = p * lax.rsqrt(ms + EPS) * gain
    partner = jnp.where(first_half, pltpu.roll(xn, WB - HEAD_DIM // 2, 1), pltpu.roll(xn, HEAD_DIM // 2, 1))
    return xn * cos + partner * sin


def _proj_kernel(prompt, nb, x_ref, ng_ref, w_ref, vg_ref, qg_ref, kg_ref, cos_ref, sin_ref, hsum_ref, *rest):
    if prompt:
        spread_ref = rest[0]
        (u_ref, va_ref, za_ref, q_ref, zb_ref, ga_ref, gb_ref,
         kaug_ref, vt_ref, km_ref, kt_ref, vtf_ref) = rest[-12:]
    else:
        u_ref, va_ref, za_ref, q_ref, zb_ref, ga_ref, gb_ref, k_ref, vb_ref = rest
    x = x_ref[...]
    ms = jnp.mean(x * x, axis=-1, keepdims=True)
    h = (x * lax.rsqrt(ms + EPS) * ng_ref[...]).astype(BF16)

    def col(i, width):
        return _dot(h, w_ref[:, OFF[i]:OFF[i] + width])

    u_ref[...] = col(0, WA).astype(u_ref.dtype)
    va = col(1, WA)
    va_ref[...] = va * lax.rsqrt(jnp.mean(va * va, axis=-1, keepdims=True) + EPS) * vg_ref[...]
    za_ref[...] = col(2, WA).astype(za_ref.dtype)

    reps = WB // LANES
    cos = jnp.concatenate([cos_ref[...]] * reps, axis=1)
    sin = jnp.concatenate([sin_ref[...]] * reps, axis=1)
    lane = lax.broadcasted_iota(jnp.int32, (1, WB), 1)
    first_half = (lane % HEAD_DIM) < (HEAD_DIM // 2)
    hsum = hsum_ref[...]
    q_ref[...] = _head_norm_rope(col(3, WB), qg_ref[...], hsum, cos, sin, first_half)
    k = _head_norm_rope(col(4, WB), kg_ref[...], hsum, cos, sin, first_half)
    vb = col(5, WB)
    zb_ref[...] = col(6, WB).astype(zb_ref.dtype)
    ga_ref[...] = col(7, D_MODEL).astype(ga_ref.dtype)
    gb_ref[...] = col(8, D_MODEL).astype(gb_ref.dtype)
    if not prompt:
        k_ref[...] = k
        vb_ref[...] = vb
    else:
        j = pl.program_id(0) % nb
        lane_a = lax.broadcasted_iota(jnp.int32, (1, AUG), 1)
        mark = jnp.where(lane_a % LANES == HEAD_DIM + j, 1.0, 0.0)
        kaug_ref[...] = (_dot_block_diag(k.astype(BF16), spread_ref[...]) + mark).astype(BF16)
        kt_ref[0, 0] = k.T
        vtf = vb.T
        vtf_ref[0, 0] = vtf
        vt = vtf.astype(BF16)
        tail = lax.broadcasted_iota(jnp.int32, (BF16_SUBLANES, vt.shape[1]), 0)
        ones_row = jnp.where(tail == 0, 1.0, 0.0).astype(BF16)
        parts = []
        for hd in range(HEADS):
            parts += [vt[hd * HEAD_DIM:(hd + 1) * HEAD_DIM], ones_row]
        vt_ref[0, 0] = jnp.concatenate(parts, axis=0)
        km_ref[0] = jnp.mean(k, axis=0, keepdims=True)


def _proj(prompt, x2d, seq, ng, w, vg, qg, kg, cos, sin, hsum, spread=None, layer=0, depth=1, kv_prev=None):
    rows = x2d.shape[0]
    tm = BLOCK
    nt = rows // tm
    tiles_per_seq = seq // tm if prompt else nt
    row = lambda w_: pl.BlockSpec((tm, w_), lambda i: (i, 0))
    full = lambda a: pl.BlockSpec(a.shape, lambda i: (0,) * a.ndim)
    tab = pl.BlockSpec((tm, LANES), lambda i: (i % tiles_per_seq, 0))
    widths = (WA, WA, WA, WB, WB, D_MODEL, D_MODEL)
    dtypes = (BF16, F32, BF16, F32, BF16, BF16, BF16)
    out_shape = [jax.ShapeDtypeStruct((rows, w_), d_) for w_, d_ in zip(widths, dtypes)]
    out_specs = [row(w_) for w_ in widths]
    args = [x2d, ng, w, vg, qg, kg, cos, sin, hsum]
    lay = functools.partial(_layer_spec, layer)
    in_specs = [row(D_MODEL), lay(ng), lay(w), lay(vg), lay(qg), lay(kg), tab, tab, full(hsum)]
    aliases = {}
    if prompt:
        args.append(spread)
        in_specs.append(full(spread))
        batch = rows // seq
        kv_shape = jax.ShapeDtypeStruct((depth, batch, WB, seq), F32)
        kv_spec = pl.BlockSpec((1, 1, WB, tm), lambda i: (layer, i // tiles_per_seq, 0, i % tiles_per_seq))
        out_shape += [jax.ShapeDtypeStruct((rows, AUG), BF16),
                      jax.ShapeDtypeStruct((nt, 1, HEADS * V_ROWS, tm), BF16),
                      jax.ShapeDtypeStruct((nt, 1, WB), F32),
                      kv_shape, kv_shape]
        out_specs += [row(AUG),
                      pl.BlockSpec((1, 1, HEADS * V_ROWS, tm), lambda i: (i, 0, 0, 0)),
                      pl.BlockSpec((1, 1, WB), lambda i: (i, 0, 0)),
                      kv_spec, kv_spec]
        aliases = {len(args): len(out_shape) - 2, len(args) + 1: len(out_shape) - 1}
        args += list(kv_prev)
        in_specs += [pl.BlockSpec(memory_space=pl.ANY)] * 2
    else:
        out_shape += [jax.ShapeDtypeStruct((rows, WB), F32)] * 2
        out_specs += [row(WB)] * 2
    return pl.pallas_call(
        functools.partial(_proj_kernel, prompt, tiles_per_seq),
        out_shape=out_shape,
        grid=(nt,),
        in_specs=in_specs,
        out_specs=out_specs,
        input_output_aliases=aliases,
        compiler_params=_params(1),
        name="proj_prompt" if prompt else "proj_sample",
    )(*args)


def _top_blocks_bias(g, valid, axis):
    idx = lax.broadcasted_iota(jnp.int32, g.shape, axis).astype(F32)
    n = float(g.shape[axis])
    g = jnp.where(valid, g, NEG_INF)
    sel = jnp.zeros(g.shape, jnp.bool_)
    for _ in range(min(TOPK, g.shape[axis])):
        m = jnp.max(g, axis=axis, keepdims=True)
        first = jnp.min(jnp.where(g == m, idx, n), axis=axis, keepdims=True)
        pick = (idx == first) & (m > NEG_INF)
        sel = sel | pick
        g = jnp.where(pick, NEG_INF, g)
    return jnp.where(sel, 0.0, MASKED).astype(F32)


def _block_diag_rows(km, heads):
    n = km.shape[0]
    t = jnp.concatenate([km] * heads, axis=0)
    r = lax.broadcasted_iota(jnp.int32, t.shape, 0)
    c = lax.broadcasted_iota(jnp.int32, t.shape, 1)
    return jnp.where((r // n) == (c // HEAD_DIM), t, jnp.zeros_like(t))


def _gate_kernel(tq, q_ref, km_ref, spread_t_ref, place_t_ref, qaug_ref):
    t = pl.program_id(1)
    nb = km_ref.shape[1]
    q = q_ref[0]
    km_hi, km_lo = _split_bf16(_block_diag_rows(km_ref[0], HEADS))
    q_hi, q_lo = _split_bf16(q)
    g = _dot_nt(km_hi, q_hi) + (_dot_nt(km_hi, q_lo) + _dot_nt(km_lo, q_hi))
    g = g.reshape(HEADS, nb, tq)
    j = lax.broadcasted_iota(jnp.int32, g.shape, 1)
    pos = t * tq + lax.broadcasted_iota(jnp.int32, g.shape, 2)
    bias = _top_blocks_bias(g, j < pos // BLOCK, 1)
    bias = jnp.where(j == pos // BLOCK, 0.0, bias).reshape(HEADS * nb, tq)
    qs = (q * QK_SCALE).astype(BF16)
    half = AUG // 2
    q_t = jnp.concatenate([_dot_nt(spread_t_ref[:half, :MXU_DEPTH], qs[:, :MXU_DEPTH]),
                           _dot_nt(spread_t_ref[half:, MXU_DEPTH:], qs[:, MXU_DEPTH:])], axis=0)
    qaug_t = q_t + _dot(place_t_ref[...], bias.astype(BF16))
    qaug_ref[0] = qaug_t.astype(BF16)


def _gate(q3, km3, spread_t, place_t):
    b, s, _ = q3.shape
    nb = km3.shape[1]
    tq = min(512, s)
    assert s % tq == 0
    full = lambda a: pl.BlockSpec(a.shape, lambda bi, t: (0,) * a.ndim)
    return pl.pallas_call(
        functools.partial(_gate_kernel, tq),
        out_shape=jax.ShapeDtypeStruct((b, AUG, s), BF16),
        grid=(b, s // tq),
        in_specs=[pl.BlockSpec((1, tq, WB), lambda bi, t: (bi, t, 0)),
                  pl.BlockSpec((1, nb, WB), lambda bi, t: (bi, 0, 0)),
                  full(spread_t), full(place_t)],
        out_specs=pl.BlockSpec((1, AUG, tq), lambda bi, t: (bi, 0, t)),
        compiler_params=_params(2),
        name="moba_gate",
    )(q3, km3, spread_t, place_t)


def _attend_kernel(q_ref, k_ref, vt_ref, o_ref, s_scr, p_scr, acc_ref, m_ref, alpha_ref, cmax_ref):
    i = pl.program_id(2)
    heads = range(ATT_HEADS)

    def scores(blk, h):
        start = pl.multiple_of(blk * BLOCK, BLOCK)
        grp = slice(h * LANES, (h + 1) * LANES)
        return _dot(k_ref[0, pl.ds(start, BLOCK), grp], q_ref[0, grp, :])

    def pv(blk, h):
        return _dot(vt_ref[blk, 0, h * V_ROWS:(h + 1) * V_ROWS, :], p_scr[h])

    kk = lax.broadcasted_iota(jnp.int32, (BLOCK, BLOCK), 0)
    qq = lax.broadcasted_iota(jnp.int32, (BLOCK, BLOCK), 1)
    causal = jnp.where(kk <= qq, 0.0, MASKED).astype(F32)
    for h in heads:
        s = scores(i, h) + causal
        s_scr[0, h] = s
        cmax_ref[0, h] = jnp.max(s, axis=0, keepdims=True)
        p_scr[h] = jnp.zeros((BLOCK, BLOCK), BF16)
        acc_ref[h] = jnp.zeros((V_ROWS, BLOCK), F32)
        m_ref[h] = jnp.full((1, BLOCK), MASKED, F32)
        alpha_ref[h] = jnp.zeros((1, BLOCK), F32)

    def block_of(t):
        return jnp.where(t <= 0, i, t - 1)

    def step(t, cur, nxt):
        prev = block_of(t - 1)
        for h in heads:
            acc_ref[h] = alpha_ref[h] * acc_ref[h] + pv(prev, h)
        for h in heads:
            s = scores(t, h)
            s_scr[nxt, h] = s
            cmax_ref[nxt, h] = jnp.max(s, axis=0, keepdims=True)
        for h in heads:
            m_old = m_ref[h]
            m_new = jnp.maximum(m_old, cmax_ref[cur, h])
            alpha_ref[h] = jnp.exp2(m_old - m_new)
            m_ref[h] = m_new
            p_scr[h] = jnp.exp2(s_scr[cur, h] - m_new).astype(BF16)

    def run(first, count, unroll):
        def body(u, carry):
            for k in range(unroll):
                step(first + unroll * u + k, k % 2, (k + 1) % 2)
            return carry
        lax.fori_loop(0, count, body, 0)

    n_steps = i + 1
    quads = n_steps // 4
    run(0, quads, 4)
    run(4 * quads, (n_steps % 4) // 2, 2)
    run(n_steps - n_steps % 2, n_steps % 2, 1)

    outs = []
    for h in heads:
        acc = alpha_ref[h] * acc_ref[h] + pv(block_of(i), h)
        outs.append(acc[:HEAD_DIM] / acc[HEAD_DIM:HEAD_DIM + 1])
    o_ref[0] = jnp.concatenate(outs, axis=0).T.astype(o_ref.dtype)


def _attend(qaug_t, kaug, vt4, nb):
    b, s, _ = kaug.shape
    return pl.pallas_call(
        _attend_kernel,
        out_shape=jax.ShapeDtypeStruct((b, s, WB), BF16),
        grid=(b, HEADS // ATT_HEADS, nb),
        in_specs=[pl.BlockSpec((1, ATT_HEADS * LANES, BLOCK), lambda bi, hg, i: (bi, hg, i)),
                  pl.BlockSpec((1, s, ATT_HEADS * LANES), lambda bi, hg, i: (bi, 0, hg)),
                  pl.BlockSpec((nb, 1, ATT_HEADS * V_ROWS, BLOCK), lambda bi, hg, i: (bi, 0, hg, 0))],
        out_specs=pl.BlockSpec((1, BLOCK, ATT_HEADS * HEAD_DIM), lambda bi, hg, i: (bi, i, hg)),
        scratch_shapes=[pltpu.VMEM((2, ATT_HEADS, BLOCK, BLOCK), F32),
                        pltpu.VMEM((ATT_HEADS, BLOCK, BLOCK), BF16),
                        pltpu.VMEM((ATT_HEADS, V_ROWS, BLOCK), F32),
                        pltpu.VMEM((ATT_HEADS, 1, BLOCK), F32),
                        pltpu.VMEM((ATT_HEADS, 1, BLOCK), F32),
                        pltpu.VMEM((2, ATT_HEADS, 1, BLOCK), F32)],
        compiler_params=_params(3),
        name="moba_attend",
    )(qaug_t, kaug, vt4)


def _sample_attend_one(q, kn, vn, kt_pages, vt_pages):
    n_pages = len(kt_pages)
    t_len = q.shape[0]
    ppb = BLOCK // kt_pages[0].shape[3]
    nf = n_pages // ppb
    qbd = _block_diag_rows(q, HEADS)
    lane = lax.broadcasted_iota(jnp.int32, (WB, nf), 1)
    kmt = jnp.zeros((WB, nf), F32)
    for jb in range(nf):
        blk = kt_pages[jb * ppb][0, 0]
        for p in range(jb * ppb + 1, (jb + 1) * ppb):
            blk = blk + kt_pages[p][0, 0]
        kmt = jnp.where(lane == jb, jnp.sum(blk, axis=1, keepdims=True) * (1.0 / BLOCK), kmt)
    yield None
    q_hi, q_lo = _split_bf16(qbd)
    km_hi, km_lo = _split_bf16(kmt)
    g = _dot(q_hi, km_hi) + (_dot(q_lo, km_hi) + _dot(q_hi, km_lo))
    bias = _top_blocks_bias(g, jnp.ones(g.shape, jnp.bool_), 1)

    yield None
    qs = (qbd * QK_SCALE).astype(BF16)
    page = kt_pages[0].shape[3]
    bias_b = [jnp.broadcast_to(bias[:, jb:jb + 1], (HEADS * t_len, page)) for jb in range(nf)]
    s_pages = [_dot(qs, kp[0, 0].astype(BF16)) + bias_b[p // ppb]
               for p, kp in enumerate(kt_pages)]
    r = lax.broadcasted_iota(jnp.int32, (HEADS * t_len, t_len), 0)
    c = lax.broadcasted_iota(jnp.int32, (HEADS * t_len, t_len), 1)
    s_own = jnp.where(c <= r % t_len, _dot_nt(qs, kn.astype(BF16)), MASKED)
    m_fold = s_pages[0]
    for s in s_pages[1:]:
        m_fold = jnp.maximum(m_fold, s)
    m = jnp.maximum(jnp.max(s_own, axis=1, keepdims=True), jnp.max(m_fold, axis=1, keepdims=True))
    yield None
    p_own = jnp.exp2(s_own - m)
    acc = jnp.zeros((HEADS * t_len, WB), F32)
    for t in range(t_len):
        acc = acc + p_own[:, t:t + 1] * vn[t:t + 1, :]
    l_fold = jnp.zeros((HEADS * t_len, page), F32)
    for s, vp in zip(s_pages, vt_pages):
        p = jnp.exp2(s - m)
        l_fold = l_fold + p
        acc = acc + _dot_nt(p.astype(BF16), vp[0, 0].astype(BF16))
    l = jnp.sum(p_own, axis=1, keepdims=True) + jnp.sum(l_fold, axis=1, keepdims=True)
    out = acc / l
    rr = lax.broadcasted_iota(jnp.int32, out.shape, 0)
    cc = lax.broadcasted_iota(jnp.int32, out.shape, 1)
    out = jnp.where((rr // t_len) == (cc // HEAD_DIM), out, 0.0)
    yield jnp.sum(out.reshape(HEADS, t_len, WB), axis=0)


SAMPLE_PHASES = 4


def _sample_attend_kernel(n_pages, t_len, pt_ref, q_ref, kn_ref, vn_ref, *rest):
    del pt_ref
    o_ref = rest[-1]
    rows = [slice(i * t_len, (i + 1) * t_len) for i in range(SAMPLES_PER_STEP)]
    runs = [_sample_attend_one(q_ref[r, :], kn_ref[r, :], vn_ref[r, :],
                               rest[2 * i * n_pages:(2 * i + 1) * n_pages],
                               rest[(2 * i + 1) * n_pages:(2 * i + 2) * n_pages])
            for i, r in enumerate(rows)]
    for _ in range(SAMPLE_PHASES - 1):
        for run in runs:
            next(run)
    for r, run in zip(rows, runs):
        o_ref[r, :] = next(run)


def _sample_attend(layer, q2, kn2, vn2, cache_kt, cache_vt, page_table, t_len):
    db, n_pages = page_table.shape
    page = cache_kt.shape[3]
    assert BLOCK % page == 0 and n_pages % (BLOCK // page) == 0, "past length must be whole MoBA blocks"
    assert t_len <= BLOCK and db % SAMPLES_PER_STEP == 0
    row = pl.BlockSpec((SAMPLES_PER_STEP * t_len, WB), lambda bi, pt: (bi, 0))

    def page_spec(i, p):
        return pl.BlockSpec((1, 1, WB, page), lambda bi, pt: (layer, pt[bi * SAMPLES_PER_STEP + i, p], 0, 0))

    pages, page_args = [], []
    for i in range(SAMPLES_PER_STEP):
        pages += [page_spec(i, p) for p in range(n_pages)] * 2
        page_args += [cache_kt] * n_pages + [cache_vt] * n_pages
    return pl.pallas_call(
        functools.partial(_sample_attend_kernel, n_pages, t_len),
        out_shape=jax.ShapeDtypeStruct((db * t_len, WB), F32),
        grid_spec=pltpu.PrefetchScalarGridSpec(
            num_scalar_prefetch=1,
            grid=(db // SAMPLES_PER_STEP,),
            in_specs=[row, row, row] + pages,
            out_specs=row,
        ),
        compiler_params=_params(1),
        name="moba_sample",
    )(page_table, q2, kn2, vn2, *page_args)


def _merge_kernel(x_ref, u_ref, va_ref, za_ref, yb_ref, zb_ref, ga_ref, gb_ref,
                  ws_ref, bt_ref, wpa_ref, wpb_ref, wo_ref, o_ref, mix_ref):
    tm = x_ref.shape[0]
    gpl = LANES // (WA // A_GROUPS)
    r = lax.broadcasted_iota(jnp.int32, (CHUNK, CHUNK), 0)
    c = lax.broadcasted_iota(jnp.int32, (CHUNK, CHUNK), 1)
    lane = lax.broadcasted_iota(jnp.int32, (CHUNK, LANES), 1)
    for ch in range(tm // CHUNK):
        rows = slice(ch * CHUNK, (ch + 1) * CHUNK)
        for lg in range(WA // LANES):
            v = va_ref[rows, lg * LANES:(lg + 1) * LANES].astype(BF16)
            ws, vs = [], []
            for gi in range(gpl):
                w = ws_ref[lg * gpl + gi]
                ws.append(jnp.where(r >= c, w, jnp.zeros_like(w)))
                vs.append(jnp.where((lane // (LANES // gpl)) == gi, v, jnp.zeros_like(v)))
            mixed = _dot(jnp.concatenate(ws, axis=1), jnp.concatenate(vs, axis=0))
            mix_ref[rows, lg * LANES:(lg + 1) * LANES] = mixed + bt_ref[:, lg * LANES:(lg + 1) * LANES]
    za = za_ref[...].astype(F32)
    zb = zb_ref[...].astype(F32)
    ya = u_ref[...].astype(F32) * mix_ref[...] * (za * jax.nn.sigmoid(za))
    yb = yb_ref[...].astype(F32) * (zb * jax.nn.sigmoid(zb))
    m = (jax.nn.sigmoid(ga_ref[...].astype(F32)) * _dot(ya.astype(BF16), wpa_ref[...])
         + jax.nn.sigmoid(gb_ref[...].astype(F32)) * _dot(yb.astype(BF16), wpb_ref[...]))
    o_ref[...] = x_ref[...] + _dot(m.astype(BF16), wo_ref[...])


def _merge(layer, x2d, u, va, za, yb, zb, ga, gb, ws, bt, wpa, wpb, wo):
    rows = x2d.shape[0]
    tm = 256
    row = lambda w_: pl.BlockSpec((tm, w_), lambda i: (i, 0))
    lay = functools.partial(_layer_spec, layer)
    return pl.pallas_call(
        _merge_kernel,
        out_shape=jax.ShapeDtypeStruct((rows, D_MODEL), F32),
        grid=(rows // tm,),
        in_specs=[row(D_MODEL), row(WA), row(WA), row(WA), row(WB), row(WB), row(D_MODEL), row(D_MODEL),
                  lay(ws), lay(bt), lay(wpa), lay(wpb), lay(wo)],
        out_specs=row(D_MODEL),
        scratch_shapes=[pltpu.VMEM((tm, WA), F32)],
        compiler_params=_params(1),
        name="merge",
    )(x2d, u, va, za, yb, zb, ga, gb, ws, bt, wpa, wpb, wo)


def _rope_tables(pos):
    half = HEAD_DIM // 2
    inv = jnp.power(jnp.float32(ROPE_THETA), -jnp.arange(half, dtype=F32) * (2.0 / HEAD_DIM))
    ang = pos.astype(F32)[:, None] * inv[None, :]
    cos, sin = jnp.cos(ang), jnp.sin(ang)
    reps = LANES // HEAD_DIM
    return (jnp.tile(jnp.concatenate([cos, cos], axis=1), (1, reps)),
            jnp.tile(jnp.concatenate([-sin, sin], axis=1), (1, reps)))


def _spread_matrix():
    m = np.zeros((WB, AUG), np.float32)
    c = np.arange(WB)
    m[c, (c // HEAD_DIM) * LANES + c % HEAD_DIM] = 1.0
    return jnp.asarray(m, BF16)


def _place_matrix_t(nb):
    m = np.zeros((AUG, HEADS * nb), np.float32)
    r = np.arange(HEADS * nb)
    m[(r // nb) * LANES + HEAD_DIM + r % nb, r] = 1.0
    return jnp.asarray(m, BF16)


def kernel(x_prompt, x_sample, cache_k, cache_v, page_table, norm_g, w_in, v_norm_g, w_s, b_s,
           q_norm_g, k_norm_g, w_pa, w_pb, w_o):
    depth = w_in.shape[0]
    b, s, _ = x_prompt.shape
    db, t_len, _ = x_sample.shape
    n_pool, page = cache_k.shape[1], cache_k.shape[2]
    past_len = page_table.shape[1] * page
    assert s % BLOCK == 0 and (db * t_len) % BLOCK == 0 and CHUNK % t_len == 0
    nb = s // BLOCK
    assert nb <= LANES - HEAD_DIM, "one spare lane per key block"

    cos_p, sin_p = _rope_tables(jnp.arange(s))
    cos_s, sin_s = _rope_tables(past_len + jnp.arange(db * t_len) % t_len)
    hsum = (jnp.arange(WB)[:, None] // HEAD_DIM == jnp.arange(WB)[None, :] // HEAD_DIM).astype(BF16)
    spread = _spread_matrix()
    spread_t = spread.T
    place_t = _place_matrix_t(nb)
    cache_kt = cache_k.transpose(0, 1, 3, 4, 2).reshape(depth, n_pool, WB, page)
    cache_vt = cache_v.transpose(0, 1, 3, 4, 2).reshape(depth, n_pool, WB, page)
    cw = WA // A_GROUPS

    ng = norm_g[:, None, :]
    vg = v_norm_g[:, None, :]
    qg = jnp.tile(q_norm_g, (1, HEADS))[:, None, :]
    kg = jnp.tile(k_norm_g, (1, HEADS))[:, None, :]
    w = w_in.astype(BF16)
    wpa, wpb, wo = w_pa.astype(BF16), w_pb.astype(BF16), w_o.astype(BF16)
    ws_p = w_s.astype(BF16)
    bt_p = jnp.repeat(b_s.transpose(0, 2, 1), cw, axis=2)
    reps = CHUNK // t_len
    corner = w_s[:, :, :t_len, :t_len]
    eye = jnp.eye(reps, dtype=F32)
    ws_s = (eye[None, None, :, None, :, None] * corner[:, :, None, :, None, :]).reshape(
        depth, A_GROUPS, CHUNK, CHUNK).astype(BF16)
    bt_s = jnp.tile(bt_p[:, :t_len], (1, reps, 1))

    xp = x_prompt.reshape(b * s, D_MODEL)
    xs = x_sample.reshape(db * t_len, D_MODEL)
    cp_l, ks_l, vs_l, cs_l = [], [], [], []
    kv_new = (jnp.zeros((depth, b, WB, s), F32), jnp.zeros((depth, b, WB, s), F32))
    for l in range(depth):
        u, va, za, q, zb, ga, gb, kaug, vt, km, kt_new, vt_new = _proj(
            True, xp, s, ng, w, vg, qg, kg, cos_p, sin_p, hsum, spread, l, depth, kv_new)
        kv_new = (kt_new, vt_new)
        qaug = _gate(q.reshape(b, s, WB), km.reshape(b, nb, WB), spread_t, place_t)
        yb = _attend(qaug, kaug.reshape(b, s, AUG), vt, nb).reshape(b * s, WB)
        xp = _merge(l, xp, u, va, za, yb, zb, ga, gb, ws_p, bt_p, wpa, wpb, wo)
        cp_l.append(va.reshape(b, s, WA)[:, s - CHUNK:])

        u, va, za, q, zb, ga, gb, k, vb = _proj(False, xs, t_len, ng, w, vg, qg, kg, cos_s, sin_s, hsum, layer=l)
        yb = _sample_attend(l, q, k, vb, cache_kt, cache_vt, page_table, t_len)
        xs = _merge(l, xs, u, va, za, yb, zb, ga, gb, ws_s, bt_s, wpa, wpb, wo)
        ks_l.append(k.reshape(db, t_len, HEADS, HEAD_DIM))
        vs_l.append(vb.reshape(db, t_len, HEADS, HEAD_DIM))
        cs_l.append(va.reshape(db, t_len, WA))

    def untranspose(t):
        return t.reshape(depth, b, HEADS, HEAD_DIM, s).transpose(0, 1, 4, 2, 3)

    return (xp.reshape(b, s, D_MODEL), xs.reshape(db, t_len, D_MODEL),
            untranspose(kv_new[0]), untranspose(kv_new[1]), jnp.stack(cp_l),
            jnp.stack(ks_l), jnp.stack(vs_l), jnp.stack(cs_l))
```

```python
import functools
import math

import jax
import jax.numpy as jnp
import numpy as np
from jax import lax
from jax.experimental import pallas as pl
from jax.experimental.pallas import tpu as pltpu

D_MODEL = 1024
WA = 512
A_GROUPS = 8
CHUNK = 128
HEADS = 8
HEAD_DIM = 64
WB = HEADS * HEAD_DIM
BLOCK = 256
TOPK = 3
ROPE_THETA = 10000.0
EPS = 1e-6
OFF = np.cumsum((0, WA, WA, WA, WB, WB, WB, WB, D_MODEL)).tolist()

LANES = 128
MXU_DEPTH = 256
AUG = HEADS * LANES
VMEM_LIMIT = 56 * 1024 * 1024
ATT_HEADS = 4
Q_BLOCKS = 2
SAMPLES_PER_STEP = 2
BF16_SUBLANES = 16
V_ROWS = HEAD_DIM + BF16_SUBLANES

F32 = jnp.float32
BF16 = jnp.bfloat16
NEG_INF = float("-inf")
MASKED = -1e30
QK_SCALE = HEAD_DIM ** -0.5 * math.log2(math.e)


def _dot(a, b):
    return jnp.dot(a, b, preferred_element_type=F32)


def _dot_nt(a, b):
    return lax.dot_general(a, b, (((1,), (1,)), ((), ())), preferred_element_type=F32)


def _dot_block_diag(a, m):
    parts = m.shape[0] // MXU_DEPTH
    nc = m.shape[1] // parts
    return jnp.concatenate(
        [_dot(a[:, i * MXU_DEPTH:(i + 1) * MXU_DEPTH], m[i * MXU_DEPTH:(i + 1) * MXU_DEPTH, i * nc:(i + 1) * nc])
         for i in range(parts)], axis=1)


def _split_bf16(x):
    hi = x.astype(BF16)
    lo = (x - hi.astype(F32)).astype(BF16)
    return hi, lo


def _params(n_axes):
    return pltpu.CompilerParams(dimension_semantics=("arbitrary",) * n_axes, vmem_limit_bytes=VMEM_LIMIT)


def _layer_spec(layer, a):
    return pl.BlockSpec((None,) + a.shape[1:], lambda i: (layer,) + (0,) * (a.ndim - 1))


def _head_norm_rope(p, gain, hsum, cos, sin, first_half):
    ms = _dot_block_diag((p * p).astype(BF16), hsum) * (1.0 / HEAD_DIM)
    xn = p * lax.rsqrt(ms + EPS) * gain
    partner = jnp.where(first_half, pltpu.roll(xn, WB - HEAD_DIM // 2, 1), pltpu.roll(xn, HEAD_DIM // 2, 1))
    return xn * cos + partner * sin


def _proj_kernel(prompt, nb, x_ref, ng_ref, w_ref, vg_ref, qg_ref, kg_ref, cos_ref, sin_ref, hsum_ref, *rest):
    if prompt:
        spread_ref = rest[0]
        (u_ref, va_ref, za_ref, q_ref, zb_ref, ga_ref, gb_ref,
         kaug_ref, vt_ref, km_ref, kt_ref, vtf_ref) = rest[-12:]
    else:
        u_ref, va_ref, za_ref, q_ref, zb_ref, ga_ref, gb_ref, k_ref, vb_ref = rest
    x = x_ref[...]
    ms = jnp.mean(x * x, axis=-1, keepdims=True)
    h = (x * lax.rsqrt(ms + EPS) * ng_ref[...]).astype(BF16)

    def col(i, width):
        return _dot(h, w_ref[:, OFF[i]:OFF[i] + width])

    u_ref[...] = col(0, WA).astype(u_ref.dtype)
    va = col(1, WA)
    va_ref[...] = va * lax.rsqrt(jnp.mean(va * va, axis=-1, keepdims=True) + EPS) * vg_ref[...]
    za_ref[...] = col(2, WA).astype(za_ref.dtype)

    reps = WB // LANES
    cos = jnp.concatenate([cos_ref[...]] * reps, axis=1)
    sin = jnp.concatenate([sin_ref[...]] * reps, axis=1)
    lane = lax.broadcasted_iota(jnp.int32, (1, WB), 1)
    first_half = (lane % HEAD_DIM) < (HEAD_DIM // 2)
    hsum = hsum_ref[...]
    q_ref[...] = _head_norm_rope(col(3, WB), qg_ref[...], hsum, cos, sin, first_half)
    k = _head_norm_rope(col(4, WB), kg_ref[...], hsum, cos, sin, first_half)
    vb = col(5, WB)
    zb_ref[...] = col(6, WB).astype(zb_ref.dtype)
    ga_ref[...] = col(7, D_MODEL).astype(ga_ref.dtype)
    gb_ref[...] = col(8, D_MODEL).astype(gb_ref.dtype)
    if not prompt:
        k_ref[...] = k
        vb_ref[...] = vb
    else:
        j = pl.program_id(0) % nb
        lane_a = lax.broadcasted_iota(jnp.int32, (1, AUG), 1)
        mark = jnp.where(lane_a % LANES == HEAD_DIM + j, 1.0, 0.0)
        kaug_ref[...] = (_dot_block_diag(k.astype(BF16), spread_ref[...]) + mark).astype(BF16)
        kt_ref[0, 0] = k.T
        vtf = vb.T
        vtf_ref[0, 0] = vtf
        vt = vtf.astype(BF16)
        tail = lax.broadcasted_iota(jnp.int32, (BF16_SUBLANES, vt.shape[1]), 0)
        ones_row = jnp.where(tail == 0, 1.0, 0.0).astype(BF16)
        parts = []
        for hd in range(HEADS):
            parts += [vt[hd * HEAD_DIM:(hd + 1) * HEAD_DIM], ones_row]
        vt_ref[0, 0] = jnp.concatenate(parts, axis=0)
        km_ref[0] = jnp.mean(k, axis=0, keepdims=True)


def _proj(prompt, x2d, seq, ng, w, vg, qg, kg, cos, sin, hsum, spread=None, layer=0, depth=1, kv_prev=None):
    rows = x2d.shape[0]
    tm = BLOCK
    nt = rows // tm
    tiles_per_seq = seq // tm if prompt else nt
    row = lambda w_: pl.BlockSpec((tm, w_), lambda i: (i, 0))
    full = lambda a: pl.BlockSpec(a.shape, lambda i: (0,) * a.ndim)
    tab = pl.BlockSpec((tm, LANES), lambda i: (i % tiles_per_seq, 0))
    widths = (WA, WA, WA, WB, WB, D_MODEL, D_MODEL)
    dtypes = (BF16, F32, BF16, F32, BF16, BF16, BF16)
    out_shape = [jax.ShapeDtypeStruct((rows, w_), d_) for w_, d_ in zip(widths, dtypes)]
    out_specs = [row(w_) for w_ in widths]
    args = [x2d, ng, w, vg, qg, kg, cos, sin, hsum]
    lay = functools.partial(_layer_spec, layer)
    in_specs = [row(D_MODEL), lay(ng), lay(w), lay(vg), lay(qg), lay(kg), tab, tab, full(hsum)]
    aliases = {}
    if prompt:
        args.append(spread)
        in_specs.append(full(spread))
        batch = rows // seq
        kv_shape = jax.ShapeDtypeStruct((depth, batch, WB, seq), F32)
        kv_spec = pl.BlockSpec((1, 1, WB, tm), lambda i: (layer, i // tiles_per_seq, 0, i % tiles_per_seq))
        out_shape += [jax.ShapeDtypeStruct((rows, AUG), BF16),
                      jax.ShapeDtypeStruct((nt, 1, HEADS * V_ROWS, tm), BF16),
                      jax.ShapeDtypeStruct((nt, 1, WB), F32),
                      kv_shape, kv_shape]
        out_specs += [row(AUG),
                      pl.BlockSpec((1, 1, HEADS * V_ROWS, tm), lambda i: (i, 0, 0, 0)),
                      pl.BlockSpec((1, 1, WB), lambda i: (i, 0, 0)),
                      kv_spec, kv_spec]
        aliases = {len(args): len(out_shape) - 2, len(args) + 1: len(out_shape) - 1}
        args += list(kv_prev)
        in_specs += [pl.BlockSpec(memory_space=pl.ANY)] * 2
    else:
        out_shape += [jax.ShapeDtypeStruct((rows, WB), F32)] * 2
        out_specs += [row(WB)] * 2
    return pl.pallas_call(
        functools.partial(_proj_kernel, prompt, tiles_per_seq),
        out_shape=out_shape,
        grid=(nt,),
        in_specs=in_specs,
        out_specs=out_specs,
        input_output_aliases=aliases,
        compiler_params=_params(1),
        name="proj_prompt" if prompt else "proj_sample",
    )(*args)


def _top_blocks_bias(g, valid, axis):
    idx = lax.broadcasted_iota(jnp.int32, g.shape, axis).astype(F32)
    n = float(g.shape[axis])
    g = jnp.where(valid, g, NEG_INF)
    sel = jnp.zeros(g.shape, jnp.bool_)
    for _ in range(min(TOPK, g.shape[axis])):
        m = jnp.max(g, axis=axis, keepdims=True)
        first = jnp.min(jnp.where(g == m, idx, n), axis=axis, keepdims=True)
        pick = (idx == first) & (m > NEG_INF)
        sel = sel | pick
        g = jnp.where(pick, NEG_INF, g)
    return jnp.where(sel, 0.0, MASKED).astype(F32)


def _block_diag_rows(km, heads):
    n = km.shape[0]
    t = jnp.concatenate([km] * heads, axis=0)
    r = lax.broadcasted_iota(jnp.int32, t.shape, 0)
    c = lax.broadcasted_iota(jnp.int32, t.shape, 1)
    return jnp.where((r // n) == (c // HEAD_DIM), t, jnp.zeros_like(t))


def _gate_kernel(tq, q_ref, km_ref, spread_t_ref, place_t_ref, qaug_ref):
    t = pl.program_id(1)
    nb = km_ref.shape[1]
    q = q_ref[0]
    km_hi, km_lo = _split_bf16(_block_diag_rows(km_ref[0], HEADS))
    q_hi, q_lo = _split_bf16(q)
    g = _dot_nt(km_hi, q_hi) + (_dot_nt(km_hi, q_lo) + _dot_nt(km_lo, q_hi))
    g = g.reshape(HEADS, nb, tq)
    j = lax.broadcasted_iota(jnp.int32, g.shape, 1)
    pos = t * tq + lax.broadcasted_iota(jnp.int32, g.shape, 2)
    bias = _top_blocks_bias(g, j < pos // BLOCK, 1)
    bias = jnp.where(j == pos // BLOCK, 0.0, bias).reshape(HEADS * nb, tq)
    qs = (q * QK_SCALE).astype(BF16)
    half = AUG // 2
    q_t = jnp.concatenate([_dot_nt(spread_t_ref[:half, :MXU_DEPTH], qs[:, :MXU_DEPTH]),
                           _dot_nt(spread_t_ref[half:, MXU_DEPTH:], qs[:, MXU_DEPTH:])], axis=0)
    qaug_t = q_t + _dot(place_t_ref[...], bias.astype(BF16))
    qaug_ref[0] = qaug_t.astype(BF16)


def _gate(q3, km3, spread_t, place_t):
    b, s, _ = q3.shape
    nb = km3.shape[1]
    tq = min(512, s)
    assert s % tq == 0
    full = lambda a: pl.BlockSpec(a.shape, lambda bi, t: (0,) * a.ndim)
    return pl.pallas_call(
        functools.partial(_gate_kernel, tq),
        out_shape=jax.ShapeDtypeStruct((b, AUG, s), BF16),
        grid=(b, s // tq),
        in_specs=[pl.BlockSpec((1, tq, WB), lambda bi, t: (bi, t, 0)),
                  pl.BlockSpec((1, nb, WB), lambda bi, t: (bi, 0, 0)),
                  full(spread_t), full(place_t)],
        out_specs=pl.BlockSpec((1, AUG, tq), lambda bi, t: (bi, 0, t)),
        compiler_params=_params(2),
        name="moba_gate",
    )(q3, km3, spread_t, place_t)


def _attend_kernel(q_ref, k_ref, vt_ref, o_ref, *scratch):
    first = pl.program_id(2) * Q_BLOCKS
    for sub in range(Q_BLOCKS):
        _attend_block(first + sub, sub, q_ref, k_ref, vt_ref, o_ref, *[r.at[sub] for r in scratch])


def _attend_block(i, sub, q_ref, k_ref, vt_ref, o_ref, s_scr, p_scr, acc_ref, m_ref, alpha_ref, cmax_ref):
    heads = range(ATT_HEADS)
    mine = slice(sub * BLOCK, (sub + 1) * BLOCK)

    def scores(blk, h):
        start = pl.multiple_of(blk * BLOCK, BLOCK)
        grp = slice(h * LANES, (h + 1) * LANES)
        return _dot(k_ref[0, pl.ds(start, BLOCK), grp], q_ref[0, grp, mine])

    def pv(blk, h):
        return _dot(vt_ref[blk, 0, h * V_ROWS:(h + 1) * V_ROWS, :], p_scr[h])

    kk = lax.broadcasted_iota(jnp.int32, (BLOCK, BLOCK), 0)
    qq = lax.broadcasted_iota(jnp.int32, (BLOCK, BLOCK), 1)
    causal = jnp.where(kk <= qq, 0.0, MASKED).astype(F32)
    for h in heads:
        s = scores(i, h) + causal
        s_scr[0, h] = s
        cmax_ref[0, h] = jnp.max(s, axis=0, keepdims=True)
        p_scr[h] = jnp.zeros((BLOCK, BLOCK), BF16)
        acc_ref[h] = jnp.zeros((V_ROWS, BLOCK), F32)
        m_ref[h] = jnp.full((1, BLOCK), MASKED, F32)
        alpha_ref[h] = jnp.zeros((1, BLOCK), F32)

    def block_of(t):
        return jnp.where(t <= 0, i, t - 1)

    def step(t, cur, nxt):
        prev = block_of(t - 1)
        for h in heads:
            acc_ref[h] = alpha_ref[h] * acc_ref[h] + pv(prev, h)
        for h in heads:
            s = scores(t, h)
            s_scr[nxt, h] = s
            cmax_ref[nxt, h] = jnp.max(s, axis=0, keepdims=True)
        for h in heads:
            m_old = m_ref[h]
            m_new = jnp.maximum(m_old, cmax_ref[cur, h])
            alpha_ref[h] = jnp.exp2(m_old - m_new)
            m_ref[h] = m_new
            p_scr[h] = jnp.exp2(s_scr[cur, h] - m_new).astype(BF16)

    def run(first, count, unroll):
        def body(u, carry):
            for k in range(unroll):
                step(first + unroll * u + k, k % 2, (k + 1) % 2)
            return carry
        lax.fori_loop(0, count, body, 0)

    n_steps = i + 1
    quads = n_steps // 4
    run(0, quads, 4)
    run(4 * quads, (n_steps % 4) // 2, 2)
    run(n_steps - n_steps % 2, n_steps % 2, 1)

    outs = []
    for h in heads:
        acc = alpha_ref[h] * acc_ref[h] + pv(block_of(i), h)
        outs.append(acc[:HEAD_DIM] / acc[HEAD_DIM:HEAD_DIM + 1])
    o_ref[0, mine, :] = jnp.concatenate(outs, axis=0).T.astype(o_ref.dtype)


def _attend(qaug_t, kaug, vt4, nb):
    b, s, _ = kaug.shape
    return pl.pallas_call(
        _attend_kernel,
        out_shape=jax.ShapeDtypeStruct((b, s, WB), BF16),
        grid=(b, HEADS // ATT_HEADS, nb // Q_BLOCKS),
        in_specs=[pl.BlockSpec((1, ATT_HEADS * LANES, Q_BLOCKS * BLOCK), lambda bi, hg, i: (bi, hg, i)),
                  pl.BlockSpec((1, s, ATT_HEADS * LANES), lambda bi, hg, i: (bi, 0, hg)),
                  pl.BlockSpec((nb, 1, ATT_HEADS * V_ROWS, BLOCK), lambda bi, hg, i: (bi, 0, hg, 0))],
        out_specs=pl.BlockSpec((1, Q_BLOCKS * BLOCK, ATT_HEADS * HEAD_DIM), lambda bi, hg, i: (bi, i, hg)),
        scratch_shapes=[pltpu.VMEM((Q_BLOCKS, 2, ATT_HEADS, BLOCK, BLOCK), F32),
                        pltpu.VMEM((Q_BLOCKS, ATT_HEADS, BLOCK, BLOCK), BF16),
                        pltpu.VMEM((Q_BLOCKS, ATT_HEADS, V_ROWS, BLOCK), F32),
                        pltpu.VMEM((Q_BLOCKS, ATT_HEADS, 1, BLOCK), F32),
                        pltpu.VMEM((Q_BLOCKS, ATT_HEADS, 1, BLOCK), F32),
                        pltpu.VMEM((Q_BLOCKS, 2, ATT_HEADS, 1, BLOCK), F32)],
        compiler_params=_params(3),
        name="moba_attend",
    )(qaug_t, kaug, vt4)


def _sample_attend_one(q, kn, vn, kt_pages, vt_pages):
    n_pages = len(kt_pages)
    t_len = q.shape[0]
    ppb = BLOCK // kt_pages[0].shape[3]
    nf = n_pages // ppb
    qbd = _block_diag_rows(q, HEADS)
    lane = lax.broadcasted_iota(jnp.int32, (WB, nf), 1)
    kmt = jnp.zeros((WB, nf), F32)
    for jb in range(nf):
        blk = kt_pages[jb * ppb][0, 0]
        for p in range(jb * ppb + 1, (jb + 1) * ppb):
            blk = blk + kt_pages[p][0, 0]
        kmt = jnp.where(lane == jb, jnp.sum(blk, axis=1, keepdims=True) * (1.0 / BLOCK), kmt)
    yield None
    q_hi, q_lo = _split_bf16(qbd)
    km_hi, km_lo = _split_bf16(kmt)
    g = _dot(q_hi, km_hi) + (_dot(q_lo, km_hi) + _dot(q_hi, km_lo))
    bias = _top_blocks_bias(g, jnp.ones(g.shape, jnp.bool_), 1)

    yield None
    qs = (qbd * QK_SCALE).astype(BF16)
    page = kt_pages[0].shape[3]
    bias_b = [jnp.broadcast_to(bias[:, jb:jb + 1], (HEADS * t_len, page)) for jb in range(nf)]
    s_pages = [_dot(qs, kp[0, 0].astype(BF16)) + bias_b[p // ppb]
               for p, kp in enumerate(kt_pages)]
    r = lax.broadcasted_iota(jnp.int32, (HEADS * t_len, t_len), 0)
    c = lax.broadcasted_iota(jnp.int32, (HEADS * t_len, t_len), 1)
    s_own = jnp.where(c <= r % t_len, _dot_nt(qs, kn.astype(BF16)), MASKED)
    m_fold = s_pages[0]
    for s in s_pages[1:]:
        m_fold = jnp.maximum(m_fold, s)
    m = jnp.maximum(jnp.max(s_own, axis=1, keepdims=True), jnp.max(m_fold, axis=1, keepdims=True))
    yield None
    p_own = jnp.exp2(s_own - m)
    acc = jnp.zeros((HEADS * t_len, WB), F32)
    for t in range(t_len):
        acc = acc + p_own[:, t:t + 1] * vn[t:t + 1, :]
    l_fold = jnp.zeros((HEADS * t_len, page), F32)
    for s, vp in zip(s_pages, vt_pages):
        p = jnp.exp2(s - m)
        l_fold = l_fold + p
        acc = acc + _dot_nt(p.astype(BF16), vp[0, 0].astype(BF16))
    l = jnp.sum(p_own, axis=1, keepdims=True) + jnp.sum(l_fold, axis=1, keepdims=True)
    out = acc / l
    rr = lax.broadcasted_iota(jnp.int32, out.shape, 0)
    cc = lax.broadcasted_iota(jnp.int32, out.shape, 1)
    out = jnp.where((rr // t_len) == (cc // HEAD_DIM), out, 0.0)
    yield jnp.sum(out.reshape(HEADS, t_len, WB), axis=0)


SAMPLE_PHASES = 4


def _sample_attend_kernel(n_pages, t_len, pt_ref, q_ref, kn_ref, vn_ref, *rest):
    del pt_ref
    o_ref = rest[-1]
    rows = [slice(i * t_len, (i + 1) * t_len) for i in range(SAMPLES_PER_STEP)]
    runs = [_sample_attend_one(q_ref[r, :], kn_ref[r, :], vn_ref[r, :],
                               rest[2 * i * n_pages:(2 * i + 1) * n_pages],
                               rest[(2 * i + 1) * n_pages:(2 * i + 2) * n_pages])
            for i, r in enumerate(rows)]
    for _ in range(SAMPLE_PHASES - 1):
        for run in runs:
            next(run)
    for r, run in zip(rows, runs):
        o_ref[r, :] = next(run)


def _sample_attend(layer, q2, kn2, vn2, cache_kt, cache_vt, page_table, t_len):
    db, n_pages = page_table.shape
    page = cache_kt.shape[3]
    assert BLOCK % page == 0 and n_pages % (BLOCK // page) == 0, "past length must be whole MoBA blocks"
    assert t_len <= BLOCK and db % SAMPLES_PER_STEP == 0
    row = pl.BlockSpec((SAMPLES_PER_STEP * t_len, WB), lambda bi, pt: (bi, 0))

    def page_spec(i, p):
        return pl.BlockSpec((1, 1, WB, page), lambda bi, pt: (layer, pt[bi * SAMPLES_PER_STEP + i, p], 0, 0))

    pages, page_args = [], []
    for i in range(SAMPLES_PER_STEP):
        pages += [page_spec(i, p) for p in range(n_pages)] * 2
        page_args += [cache_kt] * n_pages + [cache_vt] * n_pages
    return pl.pallas_call(
        functools.partial(_sample_attend_kernel, n_pages, t_len),
        out_shape=jax.ShapeDtypeStruct((db * t_len, WB), F32),
        grid_spec=pltpu.PrefetchScalarGridSpec(
            num_scalar_prefetch=1,
            grid=(db // SAMPLES_PER_STEP,),
            in_specs=[row, row, row] + pages,
            out_specs=row,
        ),
        compiler_params=_params(1),
        name="moba_sample",
    )(page_table, q2, kn2, vn2, *page_args)


def _merge_kernel(x_ref, u_ref, va_ref, za_ref, yb_ref, zb_ref, ga_ref, gb_ref,
                  ws_ref, bt_ref, wpa_ref, wpb_ref, wo_ref, o_ref, mix_ref):
    tm = x_ref.shape[0]
    gpl = LANES // (WA // A_GROUPS)
    r = lax.broadcasted_iota(jnp.int32, (CHUNK, CHUNK), 0)
    c = lax.broadcasted_iota(jnp.int32, (CHUNK, CHUNK), 1)
    lane = lax.broadcasted_iota(jnp.int32, (CHUNK, LANES), 1)
    for ch in range(tm // CHUNK):
        rows = slice(ch * CHUNK, (ch + 1) * CHUNK)
        for lg in range(WA // LANES):
            v = va_ref[rows, lg * LANES:(lg + 1) * LANES].astype(BF16)
            ws, vs = [], []
            for gi in range(gpl):
                w = ws_ref[lg * gpl + gi]
                ws.append(jnp.where(r >= c, w, jnp.zeros_like(w)))
                vs.append(jnp.where((lane // (LANES // gpl)) == gi, v, jnp.zeros_like(v)))
            mixed = _dot(jnp.concatenate(ws, axis=1), jnp.concatenate(vs, axis=0))
            mix_ref[rows, lg * LANES:(lg + 1) * LANES] = mixed + bt_ref[:, lg * LANES:(lg + 1) * LANES]
    za = za_ref[...].astype(F32)
    zb = zb_ref[...].astype(F32)
    ya = u_ref[...].astype(F32) * mix_ref[...] * (za * jax.nn.sigmoid(za))
    yb = yb_ref[...].astype(F32) * (zb * jax.nn.sigmoid(zb))
    m = (jax.nn.sigmoid(ga_ref[...].astype(F32)) * _dot(ya.astype(BF16), wpa_ref[...])
         + jax.nn.sigmoid(gb_ref[...].astype(F32)) * _dot(yb.astype(BF16), wpb_ref[...]))
    o_ref[...] = x_ref[...] + _dot(m.astype(BF16), wo_ref[...])


def _merge(layer, x2d, u, va, za, yb, zb, ga, gb, ws, bt, wpa, wpb, wo):
    rows = x2d.shape[0]
    tm = 256
    row = lambda w_: pl.BlockSpec((tm, w_), lambda i: (i, 0))
    lay = functools.partial(_layer_spec, layer)
    return pl.pallas_call(
        _merge_kernel,
        out_shape=jax.ShapeDtypeStruct((rows, D_MODEL), F32),
        grid=(rows // tm,),
        in_specs=[row(D_MODEL), row(WA), row(WA), row(WA), row(WB), row(WB), row(D_MODEL), row(D_MODEL),
                  lay(ws), lay(bt), lay(wpa), lay(wpb), lay(wo)],
        out_specs=row(D_MODEL),
        scratch_shapes=[pltpu.VMEM((tm, WA), F32)],
        compiler_params=_params(1),
        name="merge",
    )(x2d, u, va, za, yb, zb, ga, gb, ws, bt, wpa, wpb, wo)


def _rope_tables(pos):
    half = HEAD_DIM // 2
    inv = jnp.power(jnp.float32(ROPE_THETA), -jnp.arange(half, dtype=F32) * (2.0 / HEAD_DIM))
    ang = pos.astype(F32)[:, None] * inv[None, :]
    cos, sin = jnp.cos(ang), jnp.sin(ang)
    reps = LANES // HEAD_DIM
    return (jnp.tile(jnp.concatenate([cos, cos], axis=1), (1, reps)),
            jnp.tile(jnp.concatenate([-sin, sin], axis=1), (1, reps)))


def _spread_matrix():
    m = np.zeros((WB, AUG), np.float32)
    c = np.arange(WB)
    m[c, (c // HEAD_DIM) * LANES + c % HEAD_DIM] = 1.0
    return jnp.asarray(m, BF16)


def _place_matrix_t(nb):
    m = np.zeros((AUG, HEADS * nb), np.float32)
    r = np.arange(HEADS * nb)
    m[(r // nb) * LANES + HEAD_DIM + r % nb, r] = 1.0
    return jnp.asarray(m, BF16)


def kernel(x_prompt, x_sample, cache_k, cache_v, page_table, norm_g, w_in, v_norm_g, w_s, b_s,
           q_norm_g, k_norm_g, w_pa, w_pb, w_o):
    depth = w_in.shape[0]
    b, s, _ = x_prompt.shape
    db, t_len, _ = x_sample.shape
    n_pool, page = cache_k.shape[1], cache_k.shape[2]
    past_len = page_table.shape[1] * page
    assert s % BLOCK == 0 and (db * t_len) % BLOCK == 0 and CHUNK % t_len == 0
    nb = s // BLOCK
    assert nb <= LANES - HEAD_DIM, "one spare lane per key block"
    assert nb % Q_BLOCKS == 0

    cos_p, sin_p = _rope_tables(jnp.arange(s))
    cos_s, sin_s = _rope_tables(past_len + jnp.arange(db * t_len) % t_len)
    hsum = (jnp.arange(WB)[:, None] // HEAD_DIM == jnp.arange(WB)[None, :] // HEAD_DIM).astype(BF16)
    spread = _spread_matrix()
    spread_t = spread.T
    place_t = _place_matrix_t(nb)
    cache_kt = cache_k.transpose(0, 1, 3, 4, 2).reshape(depth, n_pool, WB, page)
    cache_vt = cache_v.transpose(0, 1, 3, 4, 2).reshape(depth, n_pool, WB, page)
    cw = WA // A_GROUPS

    ng = norm_g[:, None, :]
    vg = v_norm_g[:, None, :]
    qg = jnp.tile(q_norm_g, (1, HEADS))[:, None, :]
    kg = jnp.tile(k_norm_g, (1, HEADS))[:, None, :]
    w = w_in.astype(BF16)
    wpa, wpb, wo = w_pa.astype(BF16), w_pb.astype(BF16), w_o.astype(BF16)
    ws_p = w_s.astype(BF16)
    bt_p = jnp.repeat(b_s.transpose(0, 2, 1), cw, axis=2)
    reps = CHUNK // t_len
    corner = w_s[:, :, :t_len, :t_len]
    eye = jnp.eye(reps, dtype=F32)
    ws_s = (eye[None, None, :, None, :, None] * corner[:, :, None, :, None, :]).reshape(
        depth, A_GROUPS, CHUNK, CHUNK).astype(BF16)
    bt_s = jnp.tile(bt_p[:, :t_len], (1, reps, 1))

    xp = x_prompt.reshape(b * s, D_MODEL)
    xs = x_sample.reshape(db * t_len, D_MODEL)
    cp_l, ks_l, vs_l, cs_l = [], [], [], []
    kv_new = (jnp.zeros((depth, b, WB, s), F32), jnp.zeros((depth, b, WB, s), F32))
    for l in range(depth):
        u, va, za, q, zb, ga, gb, kaug, vt, km, kt_new, vt_new = _proj(
            True, xp, s, ng, w, vg, qg, kg, cos_p, sin_p, hsum, spread, l, depth, kv_new)
        kv_new = (kt_new, vt_new)
        qaug = _gate(q.reshape(b, s, WB), km.reshape(b, nb, WB), spread_t, place_t)
        yb = _attend(qaug, kaug.reshape(b, s, AUG), vt, nb).reshape(b * s, WB)
        xp = _merge(l, xp, u, va, za, yb, zb, ga, gb, ws_p, bt_p, wpa, wpb, wo)
        cp_l.append(va.reshape(b, s, WA)[:, s - CHUNK:])

        u, va, za, q, zb, ga, gb, k, vb = _proj(False, xs, t_len, ng, w, vg, qg, kg, cos_s, sin_s, hsum, layer=l)
        yb = _sample_attend(l, q, k, vb, cache_kt, cache_vt, page_table, t_len)
        xs = _merge(l, xs, u, va, za, yb, zb, ga, gb, ws_s, bt_s, wpa, wpb, wo)
        ks_l.append(k.reshape(db, t_len, HEADS, HEAD_DIM))
        vs_l.append(vb.reshape(db, t_len, HEADS, HEAD_DIM))
        cs_l.append(va.reshape(db, t_len, WA))

    def untranspose(t):
        return t.reshape(depth, b, HEADS, HEAD_DIM, s).transpose(0, 1, 4, 2, 3)

    return (xp.reshape(b, s, D_MODEL), xs.reshape(db, t_len, D_MODEL),
            untranspose(kv_new[0]), untranspose(kv_new[1]), jnp.stack(cp_l),
            jnp.stack(ks_l), jnp.stack(vs_l), jnp.stack(cs_l))
```

```python
import functools
import math

import jax
import jax.numpy as jnp
import numpy as np
from jax import lax
from jax.experimental import pallas as pl
from jax.experimental.pallas import tpu as pltpu

D_MODEL = 1024
WA = 512
A_GROUPS = 8
CHUNK = 128
HEADS = 8
HEAD_DIM = 64
WB = HEADS * HEAD_DIM
BLOCK = 256
TOPK = 3
ROPE_THETA = 10000.0
EPS = 1e-6
OFF = np.cumsum((0, WA, WA, WA, WB, WB, WB, WB, D_MODEL)).tolist()

LANES = 128
MXU_DEPTH = 256
AUG = HEADS * LANES
VMEM_LIMIT = 56 * 1024 * 1024
ATT_HEADS = 4
Q_BLOCKS = 2
SAMPLES_PER_STEP = 2
BF16_SUBLANES = 16
V_ROWS = HEAD_DIM + BF16_SUBLANES

F32 = jnp.float32
BF16 = jnp.bfloat16
NEG_INF = float("-inf")
MASKED = -1e30
QK_SCALE = HEAD_DIM ** -0.5 * math.log2(math.e)


def _dot(a, b):
    return jnp.dot(a, b, preferred_element_type=F32)


def _dot_nt(a, b):
    return lax.dot_general(a, b, (((1,), (1,)), ((), ())), preferred_element_type=F32)


def _dot_block_diag(a, m):
    parts = m.shape[0] // MXU_DEPTH
    nc = m.shape[1] // parts
    return jnp.concatenate(
        [_dot(a[:, i * MXU_DEPTH:(i + 1) * MXU_DEPTH], m[i * MXU_DEPTH:(i + 1) * MXU_DEPTH, i * nc:(i + 1) * nc])
         for i in range(parts)], axis=1)


def _split_bf16(x):
    hi = x.astype(BF16)
    lo = (x - hi.astype(F32)).astype(BF16)
    return hi, lo


def _params(n_axes):
    return pltpu.CompilerParams(dimension_semantics=("arbitrary",) * n_axes, vmem_limit_bytes=VMEM_LIMIT)


def _layer_spec(layer, a):
    return pl.BlockSpec((None,) + a.shape[1:], lambda i: (layer,) + (0,) * (a.ndim - 1))


def _head_norm_rope(p, gain, hsum, cos, sin, first_half):
    ms = _dot_block_diag((p * p).astype(BF16), hsum) * (1.0 / HEAD_DIM)
    xn = p * lax.rsqrt(ms + EPS) * gain
    partner = jnp.where(first_half, pltpu.roll(xn, WB - HEAD_DIM // 2, 1), pltpu.roll(xn, HEAD_DIM // 2, 1))
    return xn * cos + partner * sin


def _proj_kernel(prompt, nb, x_ref, ng_ref, w_ref, vg_ref, qg_ref, kg_ref, cos_ref, sin_ref, hsum_ref, *rest):
    if prompt:
        spread_ref = rest[0]
        (u_ref, va_ref, za_ref, q_ref, zb_ref, ga_ref, gb_ref,
         kaug_ref, vt_ref, km_ref, kt_ref, vtf_ref) = rest[-12:]
    else:
        u_ref, va_ref, za_ref, q_ref, zb_ref, ga_ref, gb_ref, k_ref, vb_ref = rest
    x = x_ref[...]
    ms = jnp.mean(x * x, axis=-1, keepdims=True)
    h = (x * lax.rsqrt(ms + EPS) * ng_ref[...]).astype(BF16)

    def col(i, width):
        return _dot(h, w_ref[:, OFF[i]:OFF[i] + width])

    u_ref[...] = col(0, WA).astype(u_ref.dtype)
    va = col(1, WA)
    va_ref[...] = va * lax.rsqrt(jnp.mean(va * va, axis=-1, keepdims=True) + EPS) * vg_ref[...]
    za_ref[...] = col(2, WA).astype(za_ref.dtype)

    reps = WB // LANES
    cos = jnp.concatenate([cos_ref[...]] * reps, axis=1)
    sin = jnp.concatenate([sin_ref[...]] * reps, axis=1)
    lane = lax.broadcasted_iota(jnp.int32, (1, WB), 1)
    first_half = (lane % HEAD_DIM) < (HEAD_DIM // 2)
    hsum = hsum_ref[...]
    q_ref[...] = _head_norm_rope(col(3, WB), qg_ref[...], hsum, cos, sin, first_half)
    k = _head_norm_rope(col(4, WB), kg_ref[...], hsum, cos, sin, first_half)
    vb = col(5, WB)
    zb_ref[...] = col(6, WB).astype(zb_ref.dtype)
    ga_ref[...] = col(7, D_MODEL).astype(ga_ref.dtype)
    gb_ref[...] = col(8, D_MODEL).astype(gb_ref.dtype)
    if not prompt:
        k_ref[...] = k
        vb_ref[...] = vb
    else:
        j = pl.program_id(0) % nb
        lane_a = lax.broadcasted_iota(jnp.int32, (1, AUG), 1)
        mark = jnp.where(lane_a % LANES == HEAD_DIM + j, 1.0, 0.0)
        kaug_ref[...] = (_dot_block_diag(k.astype(BF16), spread_ref[...]) + mark).astype(BF16)
        kt_ref[0, 0] = k.T
        vtf = vb.T
        vtf_ref[0, 0] = vtf
        vt = vtf.astype(BF16)
        tail = lax.broadcasted_iota(jnp.int32, (BF16_SUBLANES, vt.shape[1]), 0)
        ones_row = jnp.where(tail == 0, 1.0, 0.0).astype(BF16)
        parts = []
        for hd in range(HEADS):
            parts += [vt[hd * HEAD_DIM:(hd + 1) * HEAD_DIM], ones_row]
        vt_ref[0, 0] = jnp.concatenate(parts, axis=0)
        km_ref[0] = jnp.mean(k, axis=0, keepdims=True)


def _proj(prompt, x2d, seq, ng, w, vg, qg, kg, cos, sin, hsum, spread=None, layer=0, depth=1, kv_prev=None):
    rows = x2d.shape[0]
    tm = BLOCK
    nt = rows // tm
    tiles_per_seq = seq // tm if prompt else nt
    row = lambda w_: pl.BlockSpec((tm, w_), lambda i: (i, 0))
    full = lambda a: pl.BlockSpec(a.shape, lambda i: (0,) * a.ndim)
    tab = pl.BlockSpec((tm, LANES), lambda i: (i % tiles_per_seq, 0))
    widths = (WA, WA, WA, WB, WB, D_MODEL, D_MODEL)
    dtypes = (BF16, F32, BF16, F32, BF16, BF16, BF16)
    out_shape = [jax.ShapeDtypeStruct((rows, w_), d_) for w_, d_ in zip(widths, dtypes)]
    out_specs = [row(w_) for w_ in widths]
    args = [x2d, ng, w, vg, qg, kg, cos, sin, hsum]
    lay = functools.partial(_layer_spec, layer)
    in_specs = [row(D_MODEL), lay(ng), lay(w), lay(vg), lay(qg), lay(kg), tab, tab, full(hsum)]
    aliases = {}
    if prompt:
        args.append(spread)
        in_specs.append(full(spread))
        batch = rows // seq
        kv_shape = jax.ShapeDtypeStruct((depth, batch, WB, seq), F32)
        kv_spec = pl.BlockSpec((1, 1, WB, tm), lambda i: (layer, i // tiles_per_seq, 0, i % tiles_per_seq))
        out_shape += [jax.ShapeDtypeStruct((rows, AUG), BF16),
                      jax.ShapeDtypeStruct((nt, 1, HEADS * V_ROWS, tm), BF16),
                      jax.ShapeDtypeStruct((nt, 1, WB), F32),
                      kv_shape, kv_shape]
        out_specs += [row(AUG),
                      pl.BlockSpec((1, 1, HEADS * V_ROWS, tm), lambda i: (i, 0, 0, 0)),
                      pl.BlockSpec((1, 1, WB), lambda i: (i, 0, 0)),
                      kv_spec, kv_spec]
        aliases = {len(args): len(out_shape) - 2, len(args) + 1: len(out_shape) - 1}
        args += list(kv_prev)
        in_specs += [pl.BlockSpec(memory_space=pl.ANY)] * 2
    else:
        out_shape += [jax.ShapeDtypeStruct((rows, WB), F32)] * 2
        out_specs += [row(WB)] * 2
    return pl.pallas_call(
        functools.partial(_proj_kernel, prompt, tiles_per_seq),
        out_shape=out_shape,
        grid=(nt,),
        in_specs=in_specs,
        out_specs=out_specs,
        input_output_aliases=aliases,
        compiler_params=_params(1),
        name="proj_prompt" if prompt else "proj_sample",
    )(*args)


def _top_blocks_bias(g, valid, axis):
    idx = lax.broadcasted_iota(jnp.int32, g.shape, axis).astype(F32)
    n = float(g.shape[axis])
    g = jnp.where(valid, g, NEG_INF)
    sel = jnp.zeros(g.shape, jnp.bool_)
    for _ in range(min(TOPK, g.shape[axis])):
        m = jnp.max(g, axis=axis, keepdims=True)
        first = jnp.min(jnp.where(g == m, idx, n), axis=axis, keepdims=True)
        pick = (idx == first) & (m > NEG_INF)
        sel = sel | pick
        g = jnp.where(pick, NEG_INF, g)
    return jnp.where(sel, 0.0, MASKED).astype(F32)


def _block_diag_rows(km, heads):
    n = km.shape[0]
    t = jnp.concatenate([km] * heads, axis=0)
    r = lax.broadcasted_iota(jnp.int32, t.shape, 0)
    c = lax.broadcasted_iota(jnp.int32, t.shape, 1)
    return jnp.where((r // n) == (c // HEAD_DIM), t, jnp.zeros_like(t))


def _gate_kernel(tq, q_ref, km_ref, spread_t_ref, place_t_ref, qaug_ref):
    t = pl.program_id(1)
    nb = km_ref.shape[1]
    q = q_ref[0]
    km_hi, km_lo = _split_bf16(_block_diag_rows(km_ref[0], HEADS))
    q_hi, q_lo = _split_bf16(q)
    g = _dot_nt(km_hi, q_hi) + (_dot_nt(km_hi, q_lo) + _dot_nt(km_lo, q_hi))
    g = g.reshape(HEADS, nb, tq)
    j = lax.broadcasted_iota(jnp.int32, g.shape, 1)
    pos = t * tq + lax.broadcasted_iota(jnp.int32, g.shape, 2)
    bias = _top_blocks_bias(g, j < pos // BLOCK, 1)
    bias = jnp.where(j == pos // BLOCK, 0.0, bias).reshape(HEADS * nb, tq)
    qs = (q * QK_SCALE).astype(BF16)
    half = AUG // 2
    q_t = jnp.concatenate([_dot_nt(spread_t_ref[:half, :MXU_DEPTH], qs[:, :MXU_DEPTH]),
                           _dot_nt(spread_t_ref[half:, MXU_DEPTH:], qs[:, MXU_DEPTH:])], axis=0)
    qaug_t = q_t + _dot(place_t_ref[...], bias.astype(BF16))
    qaug_ref[0] = qaug_t.astype(BF16)


def _gate(q3, km3, spread_t, place_t):
    b, s, _ = q3.shape
    nb = km3.shape[1]
    tq = min(512, s)
    assert s % tq == 0
    full = lambda a: pl.BlockSpec(a.shape, lambda bi, t: (0,) * a.ndim)
    return pl.pallas_call(
        functools.partial(_gate_kernel, tq),
        out_shape=jax.ShapeDtypeStruct((b, AUG, s), BF16),
        grid=(b, s // tq),
        in_specs=[pl.BlockSpec((1, tq, WB), lambda bi, t: (bi, t, 0)),
                  pl.BlockSpec((1, nb, WB), lambda bi, t: (bi, 0, 0)),
                  full(spread_t), full(place_t)],
        out_specs=pl.BlockSpec((1, AUG, tq), lambda bi, t: (bi, 0, t)),
        compiler_params=_params(2),
        name="moba_gate",
    )(q3, km3, spread_t, place_t)


def _attend_kernel(q_ref, k_ref, vt_ref, o_ref, *scratch):
    first = pl.program_id(2) * Q_BLOCKS
    for sub in range(Q_BLOCKS):
        _attend_block(first + sub, sub, q_ref, k_ref, vt_ref, o_ref, *[r.at[sub] for r in scratch])


def _attend_block(i, sub, q_ref, k_ref, vt_ref, o_ref, s_scr, p_scr, acc_ref, m_ref, alpha_ref, cmax_ref):
    heads = range(ATT_HEADS)
    mine = slice(sub * BLOCK, (sub + 1) * BLOCK)

    def scores(blk, h):
        start = pl.multiple_of(blk * BLOCK, BLOCK)
        grp = slice(h * LANES, (h + 1) * LANES)
        return _dot(k_ref[0, pl.ds(start, BLOCK), grp], q_ref[0, grp, mine])

    def pv(blk, h):
        return _dot(vt_ref[blk, 0, h * V_ROWS:(h + 1) * V_ROWS, :], p_scr[h])

    kk = lax.broadcasted_iota(jnp.int32, (BLOCK, BLOCK), 0)
    qq = lax.broadcasted_iota(jnp.int32, (BLOCK, BLOCK), 1)
    causal = jnp.where(kk <= qq, 0.0, MASKED).astype(F32)
    for h in heads:
        s = scores(i, h) + causal
        s_scr[0, h] = s
        cmax_ref[0, h] = jnp.max(s, axis=0, keepdims=True)
        p_scr[h] = jnp.zeros((BLOCK, BLOCK), BF16)
        acc_ref[h] = jnp.zeros((V_ROWS, BLOCK), F32)
        m_ref[h] = jnp.full((1, BLOCK), MASKED, F32)
        alpha_ref[h] = jnp.zeros((1, BLOCK), F32)

    def block_of(t):
        return jnp.where(t <= 0, i, t - 1)

    def step(t, cur, nxt):
        prev = block_of(t - 1)
        for h in heads:
            acc_ref[h] = alpha_ref[h] * acc_ref[h] + pv(prev, h)
        for h in heads:
            s = scores(t, h)
            s_scr[nxt, h] = s
            cmax_ref[nxt, h] = jnp.max(s, axis=0, keepdims=True)
        for h in heads:
            m_old = m_ref[h]
            m_new = jnp.maximum(m_old, cmax_ref[cur, h])
            alpha_ref[h] = jnp.exp2(m_old - m_new)
            m_ref[h] = m_new
            p_scr[h] = jnp.exp2(s_scr[cur, h] - m_new).astype(BF16)

    def run(first, count, unroll):
        def body(u, carry):
            for k in range(unroll):
                step(first + unroll * u + k, k % 2, (k + 1) % 2)
            return carry
        lax.fori_loop(0, count, body, 0)

    n_steps = i + 1
    quads = n_steps // 4
    run(0, quads, 4)
    run(4 * quads, (n_steps % 4) // 2, 2)
    run(n_steps - n_steps % 2, n_steps % 2, 1)

    outs = []
    for h in heads:
        acc = alpha_ref[h] * acc_ref[h] + pv(block_of(i), h)
        outs.append(acc[:HEAD_DIM] / acc[HEAD_DIM:HEAD_DIM + 1])
    o_ref[0, mine, :] = jnp.concatenate(outs, axis=0).T.astype(o_ref.dtype)


def _attend(qaug_t, kaug, vt4, nb):
    b, s, _ = kaug.shape
    return pl.pallas_call(
        _attend_kernel,
        out_shape=jax.ShapeDtypeStruct((b, s, WB), BF16),
        grid=(b, HEADS // ATT_HEADS, nb // Q_BLOCKS),
        in_specs=[pl.BlockSpec((1, ATT_HEADS * LANES, Q_BLOCKS * BLOCK), lambda bi, hg, i: (bi, hg, i)),
                  pl.BlockSpec((1, s, ATT_HEADS * LANES), lambda bi, hg, i: (bi, 0, hg)),
                  pl.BlockSpec((nb, 1, ATT_HEADS * V_ROWS, BLOCK), lambda bi, hg, i: (bi, 0, hg, 0))],
        out_specs=pl.BlockSpec((1, Q_BLOCKS * BLOCK, ATT_HEADS * HEAD_DIM), lambda bi, hg, i: (bi, i, hg)),
        scratch_shapes=[pltpu.VMEM((Q_BLOCKS, 2, ATT_HEADS, BLOCK, BLOCK), F32),
                        pltpu.VMEM((Q_BLOCKS, ATT_HEADS, BLOCK, BLOCK), BF16),
                        pltpu.VMEM((Q_BLOCKS, ATT_HEADS, V_ROWS, BLOCK), F32),
                        pltpu.VMEM((Q_BLOCKS, ATT_HEADS, 1, BLOCK), F32),
                        pltpu.VMEM((Q_BLOCKS, ATT_HEADS, 1, BLOCK), F32),
                        pltpu.VMEM((Q_BLOCKS, 2, ATT_HEADS, 1, BLOCK), F32)],
        compiler_params=_params(3),
        name="moba_attend",
    )(qaug_t, kaug, vt4)


def _sample_attend_one(q, kn, vn, kt_pages, vt_pages):
    n_pages = len(kt_pages)
    t_len = q.shape[0]
    ppb = BLOCK // kt_pages[0].shape[3]
    nf = n_pages // ppb
    qbd = _block_diag_rows(q, HEADS)
    lane = lax.broadcasted_iota(jnp.int32, (WB, nf), 1)
    kmt = jnp.zeros((WB, nf), F32)
    for jb in range(nf):
        blk = kt_pages[jb * ppb][0, 0]
        for p in range(jb * ppb + 1, (jb + 1) * ppb):
            blk = blk + kt_pages[p][0, 0]
        kmt = jnp.where(lane == jb, jnp.sum(blk, axis=1, keepdims=True) * (1.0 / BLOCK), kmt)
    yield None
    q_hi, q_lo = _split_bf16(qbd)
    km_hi, km_lo = _split_bf16(kmt)
    g = _dot(q_hi, km_hi) + (_dot(q_lo, km_hi) + _dot(q_hi, km_lo))
    bias = _top_blocks_bias(g, jnp.ones(g.shape, jnp.bool_), 1)

    yield None
    qs = (qbd * QK_SCALE).astype(BF16)
    page = kt_pages[0].shape[3]
    bias_b = [jnp.broadcast_to(bias[:, jb:jb + 1], (HEADS * t_len, page)) for jb in range(nf)]
    s_pages = [_dot(qs, kp[0, 0].astype(BF16)) + bias_b[p // ppb]
               for p, kp in enumerate(kt_pages)]
    r = lax.broadcasted_iota(jnp.int32, (HEADS * t_len, t_len), 0)
    c = lax.broadcasted_iota(jnp.int32, (HEADS * t_len, t_len), 1)
    s_own = jnp.where(c <= r % t_len, _dot_nt(qs, kn.astype(BF16)), MASKED)
    m_fold = s_pages[0]
    for s in s_pages[1:]:
        m_fold = jnp.maximum(m_fold, s)
    m = jnp.maximum(jnp.max(s_own, axis=1, keepdims=True), jnp.max(m_fold, axis=1, keepdims=True))
    yield None
    p_own = jnp.exp2(s_own - m)
    acc = jnp.zeros((HEADS * t_len, WB), F32)
    for t in range(t_len):
        acc = acc + p_own[:, t:t + 1] * vn[t:t + 1, :]
    l_fold = jnp.zeros((HEADS * t_len, page), F32)
    for s, vp in zip(s_pages, vt_pages):
        p = jnp.exp2(s - m)
        l_fold = l_fold + p
        acc = acc + _dot_nt(p.astype(BF16), vp[0, 0].astype(BF16))
    l = jnp.sum(p_own, axis=1, keepdims=True) + jnp.sum(l_fold, axis=1, keepdims=True)
    out = acc / l
    rr = lax.broadcasted_iota(jnp.int32, out.shape, 0)
    cc = lax.broadcasted_iota(jnp.int32, out.shape, 1)
    out = jnp.where((rr // t_len) == (cc // HEAD_DIM), out, 0.0)
    yield jnp.sum(out.reshape(HEADS, t_len, WB), axis=0)


SAMPLE_PHASES = 4


def _sample_attend_kernel(n_pages, t_len, layer, pt_ref, q_ref, kn_ref, vn_ref, kt_hbm, vt_hbm, o_ref,
                          kbuf, vbuf, sem):
    s = pl.program_id(0)
    n_steps = pl.num_programs(0)
    per_step = SAMPLES_PER_STEP * n_pages

    def copies(step, slot):
        out = []
        for i in range(SAMPLES_PER_STEP):
            for p in range(n_pages):
                pg = pt_ref[step * SAMPLES_PER_STEP + i, p]
                idx = i * n_pages + p
                out.append(pltpu.make_async_copy(kt_hbm.at[layer, pg], kbuf.at[slot, idx, 0, 0], sem.at[slot]))
                out.append(pltpu.make_async_copy(vt_hbm.at[layer, pg], vbuf.at[slot, idx, 0, 0], sem.at[slot]))
        return out

    @pl.when(s == 0)
    def _():
        for c in copies(0, 0):
            c.start()

    @pl.when(s + 1 < n_steps)
    def _():
        for c in copies(s + 1, (s + 1) % 2):
            c.start()

    slot = s % 2
    for c in copies(s, slot):
        c.wait()
    assert kbuf.shape[1] == per_step
    rows = [slice(i * t_len, (i + 1) * t_len) for i in range(SAMPLES_PER_STEP)]
    runs = [_sample_attend_one(q_ref[r, :], kn_ref[r, :], vn_ref[r, :],
                               [kbuf.at[slot, i * n_pages + p] for p in range(n_pages)],
                               [vbuf.at[slot, i * n_pages + p] for p in range(n_pages)])
            for i, r in enumerate(rows)]
    for _ in range(SAMPLE_PHASES - 1):
        for run in runs:
            next(run)
    for r, run in zip(rows, runs):
        o_ref[r, :] = next(run)


def _sample_attend(layer, q2, kn2, vn2, cache_kt, cache_vt, page_table, t_len):
    db, n_pages = page_table.shape
    page = cache_kt.shape[3]
    assert BLOCK % page == 0 and n_pages % (BLOCK // page) == 0, "past length must be whole MoBA blocks"
    assert t_len <= BLOCK and db % SAMPLES_PER_STEP == 0
    row = pl.BlockSpec((SAMPLES_PER_STEP * t_len, WB), lambda bi, pt: (bi, 0))

    hbm = pl.BlockSpec(memory_space=pl.ANY)
    ring = pltpu.VMEM((2, SAMPLES_PER_STEP * n_pages, 1, 1, WB, page), F32)
    return pl.pallas_call(
        functools.partial(_sample_attend_kernel, n_pages, t_len, layer),
        out_shape=jax.ShapeDtypeStruct((db * t_len, WB), F32),
        grid_spec=pltpu.PrefetchScalarGridSpec(
            num_scalar_prefetch=1,
            grid=(db // SAMPLES_PER_STEP,),
            in_specs=[row, row, row, hbm, hbm],
            out_specs=row,
            scratch_shapes=[ring, ring, pltpu.SemaphoreType.DMA((2,))],
        ),
        compiler_params=_params(1),
        name="moba_sample",
    )(page_table, q2, kn2, vn2, cache_kt, cache_vt)


def _merge_kernel(x_ref, u_ref, va_ref, za_ref, yb_ref, zb_ref, ga_ref, gb_ref,
                  ws_ref, bt_ref, wpa_ref, wpb_ref, wo_ref, o_ref, mix_ref):
    tm = x_ref.shape[0]
    gpl = LANES // (WA // A_GROUPS)
    r = lax.broadcasted_iota(jnp.int32, (CHUNK, CHUNK), 0)
    c = lax.broadcasted_iota(jnp.int32, (CHUNK, CHUNK), 1)
    lane = lax.broadcasted_iota(jnp.int32, (CHUNK, LANES), 1)
    for ch in range(tm // CHUNK):
        rows = slice(ch * CHUNK, (ch + 1) * CHUNK)
        for lg in range(WA // LANES):
            v = va_ref[rows, lg * LANES:(lg + 1) * LANES].astype(BF16)
            ws, vs = [], []
            for gi in range(gpl):
                w = ws_ref[lg * gpl + gi]
                ws.append(jnp.where(r >= c, w, jnp.zeros_like(w)))
                vs.append(jnp.where((lane // (LANES // gpl)) == gi, v, jnp.zeros_like(v)))
            mixed = _dot(jnp.concatenate(ws, axis=1), jnp.concatenate(vs, axis=0))
            mix_ref[rows, lg * LANES:(lg + 1) * LANES] = mixed + bt_ref[:, lg * LANES:(lg + 1) * LANES]
    za = za_ref[...].astype(F32)
    zb = zb_ref[...].astype(F32)
    ya = u_ref[...].astype(F32) * mix_ref[...] * (za * jax.nn.sigmoid(za))
    yb = yb_ref[...].astype(F32) * (zb * jax.nn.sigmoid(zb))
    m = (jax.nn.sigmoid(ga_ref[...].astype(F32)) * _dot(ya.astype(BF16), wpa_ref[...])
         + jax.nn.sigmoid(gb_ref[...].astype(F32)) * _dot(yb.astype(BF16), wpb_ref[...]))
    o_ref[...] = x_ref[...] + _dot(m.astype(BF16), wo_ref[...])


def _merge(layer, x2d, u, va, za, yb, zb, ga, gb, ws, bt, wpa, wpb, wo):
    rows = x2d.shape[0]
    tm = 256
    row = lambda w_: pl.BlockSpec((tm, w_), lambda i: (i, 0))
    lay = functools.partial(_layer_spec, layer)
    return pl.pallas_call(
        _merge_kernel,
        out_shape=jax.ShapeDtypeStruct((rows, D_MODEL), F32),
        grid=(rows // tm,),
        in_specs=[row(D_MODEL), row(WA), row(WA), row(WA), row(WB), row(WB), row(D_MODEL), row(D_MODEL),
                  lay(ws), lay(bt), lay(wpa), lay(wpb), lay(wo)],
        out_specs=row(D_MODEL),
        scratch_shapes=[pltpu.VMEM((tm, WA), F32)],
        compiler_params=_params(1),
        name="merge",
    )(x2d, u, va, za, yb, zb, ga, gb, ws, bt, wpa, wpb, wo)


def _rope_tables(pos):
    half = HEAD_DIM // 2
    inv = jnp.power(jnp.float32(ROPE_THETA), -jnp.arange(half, dtype=F32) * (2.0 / HEAD_DIM))
    ang = pos.astype(F32)[:, None] * inv[None, :]
    cos, sin = jnp.cos(ang), jnp.sin(ang)
    reps = LANES // HEAD_DIM
    return (jnp.tile(jnp.concatenate([cos, cos], axis=1), (1, reps)),
            jnp.tile(jnp.concatenate([-sin, sin], axis=1), (1, reps)))


def _spread_matrix():
    m = np.zeros((WB, AUG), np.float32)
    c = np.arange(WB)
    m[c, (c // HEAD_DIM) * LANES + c % HEAD_DIM] = 1.0
    return jnp.asarray(m, BF16)


def _place_matrix_t(nb):
    m = np.zeros((AUG, HEADS * nb), np.float32)
    r = np.arange(HEADS * nb)
    m[(r // nb) * LANES + HEAD_DIM + r % nb, r] = 1.0
    return jnp.asarray(m, BF16)


def kernel(x_prompt, x_sample, cache_k, cache_v, page_table, norm_g, w_in, v_norm_g, w_s, b_s,
           q_norm_g, k_norm_g, w_pa, w_pb, w_o):
    depth = w_in.shape[0]
    b, s, _ = x_prompt.shape
    db, t_len, _ = x_sample.shape
    n_pool, page = cache_k.shape[1], cache_k.shape[2]
    past_len = page_table.shape[1] * page
    assert s % BLOCK == 0 and (db * t_len) % BLOCK == 0 and CHUNK % t_len == 0
    nb = s // BLOCK
    assert nb <= LANES - HEAD_DIM, "one spare lane per key block"
    assert nb % Q_BLOCKS == 0

    cos_p, sin_p = _rope_tables(jnp.arange(s))
    cos_s, sin_s = _rope_tables(past_len + jnp.arange(db * t_len) % t_len)
    hsum = (jnp.arange(WB)[:, None] // HEAD_DIM == jnp.arange(WB)[None, :] // HEAD_DIM).astype(BF16)
    spread = _spread_matrix()
    spread_t = spread.T
    place_t = _place_matrix_t(nb)
    cache_kt = cache_k.transpose(0, 1, 3, 4, 2).reshape(depth, n_pool, WB, page)
    cache_vt = cache_v.transpose(0, 1, 3, 4, 2).reshape(depth, n_pool, WB, page)
    cw = WA // A_GROUPS

    ng = norm_g[:, None, :]
    vg = v_norm_g[:, None, :]
    qg = jnp.tile(q_norm_g, (1, HEADS))[:, None, :]
    kg = jnp.tile(k_norm_g, (1, HEADS))[:, None, :]
    w = w_in.astype(BF16)
    wpa, wpb, wo = w_pa.astype(BF16), w_pb.astype(BF16), w_o.astype(BF16)
    ws_p = w_s.astype(BF16)
    bt_p = jnp.repeat(b_s.transpose(0, 2, 1), cw, axis=2)
    reps = CHUNK // t_len
    corner = w_s[:, :, :t_len, :t_len]
    eye = jnp.eye(reps, dtype=F32)
    ws_s = (eye[None, None, :, None, :, None] * corner[:, :, None, :, None, :]).reshape(
        depth, A_GROUPS, CHUNK, CHUNK).astype(BF16)
    bt_s = jnp.tile(bt_p[:, :t_len], (1, reps, 1))

    xp = x_prompt.reshape(b * s, D_MODEL)
    xs = x_sample.reshape(db * t_len, D_MODEL)
    cp_l, ks_l, vs_l, cs_l = [], [], [], []
    kv_new = (jnp.zeros((depth, b, WB, s), F32), jnp.zeros((depth, b, WB, s), F32))
    for l in range(depth):
        u, va, za, q, zb, ga, gb, kaug, vt, km, kt_new, vt_new = _proj(
            True, xp, s, ng, w, vg, qg, kg, cos_p, sin_p, hsum, spread, l, depth, kv_new)
        kv_new = (kt_new, vt_new)
        qaug = _gate(q.reshape(b, s, WB), km.reshape(b, nb, WB), spread_t, place_t)
        yb = _attend(qaug, kaug.reshape(b, s, AUG), vt, nb).reshape(b * s, WB)
        xp = _merge(l, xp, u, va, za, yb, zb, ga, gb, ws_p, bt_p, wpa, wpb, wo)
        cp_l.append(va.reshape(b, s, WA)[:, s - CHUNK:])

        u, va, za, q, zb, ga, gb, k, vb = _proj(False, xs, t_len, ng, w, vg, qg, kg, cos_s, sin_s, hsum, layer=l)
        yb = _sample_attend(l, q, k, vb, cache_kt, cache_vt, page_table, t_len)
        xs = _merge(l, xs, u, va, za, yb, zb, ga, gb, ws_s, bt_s, wpa, wpb, wo)
        ks_l.append(k.reshape(db, t_len, HEADS, HEAD_DIM))
        vs_l.append(vb.reshape(db, t_len, HEADS, HEAD_DIM))
        cs_l.append(va.reshape(db, t_len, WA))

    def untranspose(t):
        return t.reshape(depth, b, HEADS, HEAD_DIM, s).transpose(0, 1, 4, 2, 3)

    return (xp.reshape(b, s, D_MODEL), xs.reshape(db, t_len, D_MODEL),
            untranspose(kv_new[0]), untranspose(kv_new[1]), jnp.stack(cp_l),
            jnp.stack(ks_l), jnp.stack(vs_l), jnp.stack(cs_l))
```
